```python
import jax, jax.numpy as jnp
from jax import lax
import numpy as np

D_MODEL = 4096
BATCH = 16
SEQ = 256
DEPTH = 2
DEC_BATCH = 2
DEC_SEQ = 2048
PAST_LEN = 256

GRID_W = 64
N_FOURIER_GROUPS = 4
FOURIER_GROUP = 256
D_FOURIER = N_FOURIER_GROUPS * FOURIER_GROUP
D_CONV = 1024
N_HEADS = 16
Q_LORA = 1024
KV_LORA = 512
QK_NOPE = 128
QK_ROPE = 64
V_HEAD = 128
ROPE_BASE = 10000.0
Q_BLOCK = 128
N_BRANCH = 3
D_FF = 11008
N_EXPERTS = 8
TOP_K = 2
D_FF_EXPERT = 14336
N_DENSE = (DEPTH + 1) // 2
N_MOE = DEPTH // 2
EPS = 1e-6
IN_COLS = D_FOURIER + 3 * D_CONV + Q_LORA + KV_LORA + QK_ROPE + N_BRANCH * D_MODEL

kernel_name = "hybrid_diffusion_fourier_conv_mla_step"


def rmsnorm(x, g):
    xf = x.astype(jnp.float32)
    y = xf * lax.rsqrt(jnp.mean(xf * xf, axis=-1, keepdims=True) + EPS)
    return (y * g.astype(jnp.float32)).astype(x.dtype)


def adaln(cond, w_ada_l, b_ada_l):
    m = jax.nn.silu(cond) @ w_ada_l + b_ada_l
    return jnp.split(m[:, None, :], 6, axis=-1)


def axial_rope_tables(n_tokens):
    n_rows = n_tokens // GRID_W
    row = jnp.repeat(jnp.arange(n_rows), GRID_W).astype(jnp.float32)
    col = jnp.tile(jnp.arange(GRID_W), n_rows).astype(jnp.float32)
    half = QK_ROPE // 2
    inv = ROPE_BASE ** (-(jnp.arange(half // 2, dtype=jnp.float32) * 2.0) / half)
    ang_r = row[:, None] * inv
    ang_c = col[:, None] * inv
    ang = jnp.concatenate([ang_r, ang_r, ang_c, ang_c], axis=-1)
    return jnp.cos(ang), jnp.sin(ang)


def apply_axial_rope(x, cos, sin):
    x1, x2, x3, x4 = jnp.split(x, 4, axis=-1)
    x_rot = jnp.concatenate([-x2, x1, -x4, x3], axis=-1)
    return (x.astype(jnp.float32) * cos + x_rot.astype(jnp.float32) * sin).astype(x.dtype)


def fourier_mix(u):
    b, l, _ = u.shape
    ug = u.astype(jnp.float32).reshape(b, l, N_FOURIER_GROUPS, FOURIER_GROUP)
    f = jnp.fft.fft2(ug, axes=(1, 3), norm="ortho").real
    return f.reshape(b, l, D_FOURIER).astype(u.dtype)


def short_conv_mix(b_g, c_g, h_c, w):
    u = c_g * h_c
    zero = jnp.zeros_like(u[:, :1])
    prev = jnp.concatenate([zero, u[:, :-1]], axis=1)
    nxt = jnp.concatenate([u[:, 1:], zero], axis=1)
    return b_g * (w[0] * prev + w[1] * u + w[2] * nxt)


def mla_attend(q_nope, q_pe, k_nope, k_pe, v):
    b, lq, h, _ = q_nope.shape
    nb = lq // Q_BLOCK
    scale = (QK_NOPE + QK_ROPE) ** -0.5

    def block(qs):
        qn, qp = qs
        s = (jnp.einsum("bqhd,bkhd->bhqk", qn, k_nope, preferred_element_type=jnp.float32)
             + jnp.einsum("bqhr,bkr->bhqk", qp, k_pe, preferred_element_type=jnp.float32))
        p = jax.nn.softmax(s * scale, axis=-1)
        return jnp.einsum("bhqk,bkhd->bqhd", p.astype(v.dtype), v)

    qn_b = q_nope.reshape(b, nb, Q_BLOCK, h, QK_NOPE).transpose(1, 0, 2, 3, 4)
    qp_b = q_pe.reshape(b, nb, Q_BLOCK, h, QK_ROPE).transpose(1, 0, 2, 3, 4)
    o = lax.map(block, (qn_b, qp_b))
    return o.transpose(1, 0, 2, 3, 4).reshape(b, lq, h * V_HEAD)


def token_mixer(h, p, i, rope, ctx_ckv, ctx_kpe):
    b, l, _ = h.shape
    z = h @ p["w_in"][i]
    sizes = (D_FOURIER, D_CONV, D_CONV, D_CONV, Q_LORA, KV_LORA, QK_ROPE)
    idx, acc = [], 0
    for s in sizes:
        acc += s
        idx.append(acc)
    u_f, b_g, c_g, h_c, q_lat, kv_lat, k_pe, gate_logits = jnp.split(z, idx, axis=-1)
    y_f = fourier_mix(u_f) @ p["w_fourier_out"][i]
    y_c = short_conv_mix(b_g, c_g, h_c, p["conv_w"][i]) @ p["w_conv_out"][i]
    q = (rmsnorm(q_lat, p["q_norm_g"][i]) @ p["w_uq"][i]).reshape(b, l, N_HEADS, QK_NOPE + QK_ROPE)
    q_nope, q_pe = q[..., :QK_NOPE], q[..., QK_NOPE:]
    c_kv = rmsnorm(kv_lat, p["kv_norm_g"][i])
    k_pe_own = k_pe
    if rope is not None:
        cos, sin = rope
        q_pe = apply_axial_rope(q_pe, cos[None, :, None, :], sin[None, :, None, :])
        k_pe = apply_axial_rope(k_pe, cos[None], sin[None])
    if ctx_ckv is not None:
        ckv_all = jnp.concatenate([ctx_ckv.astype(c_kv.dtype), c_kv], axis=1)
        kpe_all = jnp.concatenate([ctx_kpe.astype(k_pe.dtype), k_pe], axis=1)
    else:
        ckv_all, kpe_all = c_kv, k_pe
    kv = (ckv_all @ p["w_ukv"][i]).reshape(b, -1, N_HEADS, QK_NOPE + V_HEAD)
    k_nope, v = kv[..., :QK_NOPE], kv[..., QK_NOPE:]
    y_a = mla_attend(q_nope, q_pe, k_nope, kpe_all, v) @ p["w_mla_out"][i]
    g = jax.nn.sigmoid(gate_logits.astype(jnp.float32)).astype(h.dtype).reshape(b, l, N_BRANCH, D_MODEL)
    merged = g[:, :, 0] * y_f + g[:, :, 1] * y_c + g[:, :, 2] * y_a
    return merged @ p["w_o"][i], c_kv, k_pe_own


def swiglu(h, w1, w3, w2):
    return (jax.nn.silu(h @ w1) * (h @ w3)) @ w2


def moe_swiglu(h, router_w, router_b, e_w1, e_w3, e_w2):
    b, l, d = h.shape
    hf = h.reshape(-1, d)
    logits = (hf @ router_w).astype(jnp.float32) + router_b.astype(jnp.float32)
    top_v, top_i = lax.top_k(logits, TOP_K)
    wts = jax.nn.softmax(top_v, axis=-1)
    gate = jnp.sum(jax.nn.one_hot(top_i, N_EXPERTS, dtype=jnp.float32) * wts[..., None], axis=1)
    out = jnp.zeros_like(hf)
    for e in range(N_EXPERTS):
        out = out + gate[:, e:e + 1].astype(hf.dtype) * swiglu(hf, e_w1[e], e_w3[e], e_w2[e])
    return out.reshape(b, l, d)


def trunk_layer(x, cond, p, i, rope, ctx_ckv, ctx_kpe):
    sh1, sc1, g1, sh2, sc2, g2 = adaln(cond, p["w_ada"][i], p["b_ada"][i])
    h = rmsnorm(x, p["norm1_g"][i]) * (1 + sc1) + sh1
    mix, c_kv, k_pe = token_mixer(h, p, i, rope, ctx_ckv, ctx_kpe)
    x = x + g1 * mix
    h = rmsnorm(x, p["norm2_g"][i]) * (1 + sc2) + sh2
    if i % 2 == 0:
        j = i // 2
        f = swiglu(h, p["ffn_w1"][j], p["ffn_w3"][j], p["ffn_w2"][j])
    else:
        j = i // 2
        f = moe_swiglu(h, p["router_w"][j], p["router_b"][j], p["moe_w1"][j], p["moe_w3"][j], p["moe_w2"][j])
    x = x + g2 * f
    return x, c_kv, k_pe


def setup_inputs(seed: int = 0) -> dict:
    key = jax.random.key(seed)
    ks = iter(jax.random.split(key, 40))

    def nrm(shape, scale):
        return jax.random.normal(next(ks), shape, jnp.float32) * scale

    def gain(shape):
        return 1.0 + nrm(shape, 0.02)

    return {
        "x_prompt": nrm((BATCH, SEQ, D_MODEL), 1.0),
        "x_sample": nrm((DEC_BATCH, DEC_SEQ, D_MODEL), 1.0),
        "cache_ckv": nrm((DEC_BATCH, DEPTH, PAST_LEN, KV_LORA), 1.0),
        "cache_kpe": nrm((DEC_BATCH, DEPTH, PAST_LEN, QK_ROPE), 1.0),
        "c": nrm((DEC_BATCH, D_MODEL), 1.0),
        "c_ctx": nrm((D_MODEL,), 1.0),
        "norm1_g": gain((DEPTH, D_MODEL)),
        "norm2_g": gain((DEPTH, D_MODEL)),
        "w_ada": nrm((DEPTH, D_MODEL, 6 * D_MODEL), 0.5 * D_MODEL ** -0.5),
        "b_ada": nrm((DEPTH, 6 * D_MODEL), 0.02),
        "w_in": nrm((DEPTH, D_MODEL, IN_COLS), D_MODEL ** -0.5),
        "w_fourier_out": nrm((DEPTH, D_FOURIER, D_MODEL), D_FOURIER ** -0.5),
        "conv_w": nrm((DEPTH, 3, D_CONV), 3 ** -0.5),
        "w_conv_out": nrm((DEPTH, D_CONV, D_MODEL), D_CONV ** -0.5),
        "q_norm_g": gain((DEPTH, Q_LORA)),
        "w_uq": nrm((DEPTH, Q_LORA, N_HEADS * (QK_NOPE + QK_ROPE)), Q_LORA ** -0.5),
        "kv_norm_g": gain((DEPTH, KV_LORA)),
        "w_ukv": nrm((DEPTH, KV_LORA, N_HEADS * (QK_NOPE + V_HEAD)), KV_LORA ** -0.5),
        "w_mla_out": nrm((DEPTH, N_HEADS * V_HEAD, D_MODEL), (N_HEADS * V_HEAD) ** -0.5),
        "w_o": nrm((DEPTH, D_MODEL, D_MODEL), D_MODEL ** -0.5),
        "ffn_w1": nrm((N_DENSE, D_MODEL, D_FF), D_MODEL ** -0.5),
        "ffn_w3": nrm((N_DENSE, D_MODEL, D_FF), D_MODEL ** -0.5),
        "ffn_w2": nrm((N_DENSE, D_FF, D_MODEL), D_FF ** -0.5),
        "router_w": nrm((N_MOE, D_MODEL, N_EXPERTS), D_MODEL ** -0.5),
        "router_b": nrm((N_MOE, N_EXPERTS), 0.01),
        "moe_w1": nrm((N_MOE, N_EXPERTS, D_MODEL, D_FF_EXPERT), D_MODEL ** -0.5),
        "moe_w3": nrm((N_MOE, N_EXPERTS, D_MODEL, D_FF_EXPERT), D_MODEL ** -0.5),
        "moe_w2": nrm((N_MOE, N_EXPERTS, D_FF_EXPERT, D_MODEL), D_FF_EXPERT ** -0.5),
        "final_g": gain((D_MODEL,)),
    }


def reference(x_prompt, x_sample, cache_ckv, cache_kpe, c, c_ctx,
              norm1_g, norm2_g, w_ada, b_ada, w_in, w_fourier_out, conv_w, w_conv_out,
              q_norm_g, w_uq, kv_norm_g, w_ukv, w_mla_out, w_o,
              ffn_w1, ffn_w3, ffn_w2, router_w, router_b, moe_w1, moe_w3, moe_w2, final_g):
    p = {
        "norm1_g": norm1_g, "norm2_g": norm2_g, "w_ada": w_ada, "b_ada": b_ada,
        "w_in": w_in, "w_fourier_out": w_fourier_out, "conv_w": conv_w, "w_conv_out": w_conv_out,
        "q_norm_g": q_norm_g, "w_uq": w_uq, "kv_norm_g": kv_norm_g, "w_ukv": w_ukv,
        "w_mla_out": w_mla_out, "w_o": w_o,
        "ffn_w1": ffn_w1, "ffn_w3": ffn_w3, "ffn_w2": ffn_w2,
        "router_w": router_w, "router_b": router_b, "moe_w1": moe_w1, "moe_w3": moe_w3, "moe_w2": moe_w2,
    }
    xc = x_prompt
    cond_ctx = c_ctx[None, :]
    ckv_list, kpe_list = [], []
    for i in range(DEPTH):
        xc, ckv_i, kpe_i = trunk_layer(xc, cond_ctx, p, i, None, None, None)
        ckv_list.append(ckv_i)
        kpe_list.append(kpe_i)
    y_prompt = rmsnorm(xc, final_g)
    new_ckv = jnp.stack(ckv_list, axis=1)
    new_kpe = jnp.stack(kpe_list, axis=1)
    rope = axial_rope_tables(x_sample.shape[1])
    xs = x_sample
    for i in range(DEPTH):
        xs, _, _ = trunk_layer(xs, c, p, i, rope, cache_ckv[:, i], cache_kpe[:, i])
    y_sample = rmsnorm(xs, final_g)
    return (y_prompt, y_sample, new_ckv, new_kpe)
```

```python
import functools

import jax
import jax.numpy as jnp
import numpy as np
from jax import lax
from jax.experimental import pallas as pl
from jax.experimental.pallas import tpu as pltpu

F32 = jnp.float32
BF16 = jnp.bfloat16

D_MODEL = 4096
BATCH = 16
SEQ = 256
DEPTH = 2
DEC_BATCH = 2
DEC_SEQ = 2048
PAST_LEN = 256
GRID_W = 64
N_FOURIER_GROUPS = 4
FOURIER_GROUP = 256
D_FOURIER = N_FOURIER_GROUPS * FOURIER_GROUP
D_CONV = 1024
N_HEADS = 16
Q_LORA = 1024
KV_LORA = 512
QK_NOPE = 128
QK_ROPE = 64
V_HEAD = 128
ROPE_BASE = 10000.0
N_BRANCH = 3
D_FF = 11008
N_EXPERTS = 8
TOP_K = 2
D_FF_EXPERT = 14336
EPS = 1e-6

N_PROMPT = BATCH * SEQ
N_SAMPLE = DEC_BATCH * DEC_SEQ
N_TOK = N_PROMPT + N_SAMPLE
KV_LEN = PAST_LEN + DEC_SEQ
N_KV_ROWS = DEC_BATCH * KV_LEN + N_PROMPT
GATE_COL0 = D_FOURIER + 3 * D_CONV + Q_LORA + KV_LORA + QK_ROPE
Z_MAIN = 6144
HEAD_PAD = 256
LANES = 128

VMEM_LIMIT = 56 * 1024 * 1024
MOE_BM = 1024
MOE_TILES = (TOP_K * N_TOK) // MOE_BM + N_EXPERTS
GATHER_ROWS = 256


def _params(n_axes, vmem=VMEM_LIMIT):
    return pltpu.CompilerParams(dimension_semantics=("arbitrary",) * n_axes, vmem_limit_bytes=vmem)


def _cond_of_tile(i, bm):
    r0 = i * bm
    return jnp.where(r0 < N_PROMPT, 0, 1 + (r0 - N_PROMPT) // DEC_SEQ)


def _split_bf16(x):
    hi = x.astype(BF16)
    lo = (x - hi.astype(F32)).astype(BF16)
    return hi, lo


def _dot(a, b):
    return jnp.dot(a, b, preferred_element_type=F32)


def _dot3(a_hi, a_lo, b_hi, b_lo):
    return _dot(a_hi, b_hi) + _dot(a_hi, b_lo) + _dot(a_lo, b_hi)


def _ada_body(c_ref, w_ref, b_ref, o_ref):
    c = c_ref[...]
    s = (c * jax.nn.sigmoid(c)).astype(BF16)
    o_ref[...] = _dot(s, w_ref[...].astype(BF16)) + b_ref[...]


def _ada(cond8, w_ada, b_ada, bn=512):
    n = w_ada.shape[-1]
    return pl.pallas_call(
        _ada_body,
        grid=(DEPTH, n // bn),
        in_specs=[
            pl.BlockSpec((8, D_MODEL), lambda l, j: (0, 0)),
            pl.BlockSpec((None, D_MODEL, bn), lambda l, j: (l, 0, j)),
            pl.BlockSpec((None, 1, bn), lambda l, j: (l, 0, j)),
        ],
        out_specs=pl.BlockSpec((None, 8, bn), lambda l, j: (l, 0, j)),
        out_shape=jax.ShapeDtypeStruct((DEPTH, 8, n), F32),
        compiler_params=_params(2),
        name="ada",
    )(cond8, w_ada, b_ada.reshape(DEPTH, 1, n))


def _rms(x):
    return x * lax.rsqrt(jnp.mean(x * x, axis=-1, keepdims=True) + EPS)


def _norm_mod_body(x_ref, g_ref, sc_ref, sh_ref, o_ref):
    y = _rms(x_ref[...]) * g_ref[...]
    o_ref[...] = (y * (1 + sc_ref[...]) + sh_ref[...]).astype(o_ref.dtype)


def _norm_mod_router_body(x_ref, g_ref, sc_ref, sh_ref, rw_hi_ref, rw_lo_ref, rb_ref,
                          of_ref, wt_ref, ix_ref):
    y = _rms(x_ref[...]) * g_ref[...]
    h = y * (1 + sc_ref[...]) + sh_ref[...]
    of_ref[...] = h
    h_hi, h_lo = _split_bf16(h)
    logits = _dot3(h_hi, h_lo, rw_hi_ref[...], rw_lo_ref[...]) + rb_ref[...]
    lane = lax.broadcasted_iota(jnp.int32, logits.shape, 1)
    lanef = lane.astype(F32)
    neg = jnp.float32(-jnp.inf)
    lg = jnp.where(lane < N_EXPERTS, logits, neg)
    m1 = jnp.max(lg, axis=-1, keepdims=True)
    i1 = jnp.min(jnp.where(lg == m1, lanef, float(LANES)), axis=-1, keepdims=True)
    lg2 = jnp.where(lanef == i1, neg, lg)
    m2 = jnp.max(lg2, axis=-1, keepdims=True)
    i2 = jnp.min(jnp.where(lg2 == m2, lanef, float(LANES)), axis=-1, keepdims=True)
    e = jnp.exp(m2 - m1)
    w1 = 1.0 / (1.0 + e)
    w2 = e / (1.0 + e)
    wt_ref[...] = jnp.where(lane == 0, w1, jnp.where(lane == 1, w2, 0.0))
    ix_ref[...] = jnp.where(lane == 0, i1, jnp.where(lane == 1, i2, 0.0)).astype(jnp.int32)


def _norm_mod(x, gain, layer, mods, k_scale, k_shift, router=None, bm=256):
    m = x.shape[0]
    in_specs = [
        pl.BlockSpec((bm, D_MODEL), lambda i: (i, 0)),
        pl.BlockSpec((None, 1, D_MODEL), lambda i: (layer, 0, 0)),
        pl.BlockSpec((None, 1, D_MODEL), lambda i: (_cond_of_tile(i, bm) * 6 + k_scale, 0, 0)),
        pl.BlockSpec((None, 1, D_MODEL), lambda i: (_cond_of_tile(i, bm) * 6 + k_shift, 0, 0)),
    ]
    args = [x, gain.reshape(DEPTH, 1, D_MODEL), mods, mods]
    row = pl.BlockSpec((bm, D_MODEL), lambda i: (i, 0))
    if router is None:
        return pl.pallas_call(
            _norm_mod_body, grid=(m // bm,), in_specs=in_specs, out_specs=row,
            out_shape=jax.ShapeDtypeStruct((m, D_MODEL), BF16),
            compiler_params=_params(1), name="norm_mod",
        )(*args)
    rw_hi, rw_lo, rb = router
    small = pl.BlockSpec((bm, LANES), lambda i: (i, 0))
    in_specs += [
        pl.BlockSpec((D_MODEL, LANES), lambda i: (0, 0)),
        pl.BlockSpec((D_MODEL, LANES), lambda i: (0, 0)),
        pl.BlockSpec((1, LANES), lambda i: (0, 0)),
    ]
    return pl.pallas_call(
        _norm_mod_router_body, grid=(m // bm,), in_specs=in_specs,
        out_specs=[row, small, small],
        out_shape=[jax.ShapeDtypeStruct((m, D_MODEL), F32),
                   jax.ShapeDtypeStruct((m, LANES), F32), jax.ShapeDtypeStruct((m, LANES), jnp.int32)],
        compiler_params=_params(1), name="norm_mod_router",
    )(*args, rw_hi, rw_lo, rb)


def _rmsnorm_body(x_ref, g_ref, o_ref):
    o_ref[...] = (_rms(x_ref[...]) * g_ref[...]).astype(o_ref.dtype)


def _rmsnorm(x, gain3, lead, width, col_blk, bm=512):
    m = x.shape[0]
    return pl.pallas_call(
        _rmsnorm_body, grid=(m // bm,),
        in_specs=[pl.BlockSpec((bm, width), lambda i: (i, col_blk)),
                  pl.BlockSpec((None, 1, width), lambda i: (lead, 0, 0))],
        out_specs=pl.BlockSpec((bm, width), lambda i: (i, 0)),
        out_shape=jax.ShapeDtypeStruct((m, width), F32),
        compiler_params=_params(1), name="rmsnorm",
    )(x, gain3)


def _mm(name, x, ws, epilogue, *, bm, bn, n_out, out_dtype, k=None, x_kblk=0, extras=(),
        prologue=None, pro_extras=()):
    m = x.shape[0]
    k = k or x.shape[1]
    n_w, n_e, n_p = len(ws), len(extras), len(pro_extras)
    in_specs = [pl.BlockSpec((bm, k), lambda i, j: (i, x_kblk))]
    args = [x]
    for arr, blk, imap in pro_extras:
        in_specs.append(pl.BlockSpec(blk, imap))
        args.append(arr)
    for arr, lead, coff in ws:
        in_specs.append(pl.BlockSpec((None,) * len(lead) + (k, bn),
                                     lambda i, j, lead=lead, coff=coff: lead + (0, j + coff)))
        args.append(arr)
    for arr, blk, imap in extras:
        in_specs.append(pl.BlockSpec(blk, imap))
        args.append(arr)

    def body(*refs):
        x_ref = refs[0]
        p_refs = refs[1:1 + n_p]
        w_refs = refs[1 + n_p:1 + n_p + n_w]
        e_refs = refs[1 + n_p + n_w:1 + n_p + n_w + n_e]
        o_ref = refs[1 + n_p + n_w + n_e]
        if prologue is None:
            xv = x_ref[...]
        else:
            xs_ref = refs[-1]

            @pl.when(pl.program_id(1) == 0)
            def _():
                xs_ref[...] = prologue(x_ref, *p_refs)

            xv = xs_ref[...]
        accs = [_dot(xv, w[...].astype(BF16)) for w in w_refs]
        o_ref[...] = epilogue(accs, *e_refs).astype(o_ref.dtype)

    scratch = [] if prologue is None else [pltpu.VMEM((bm, k), BF16)]
    return pl.pallas_call(
        body, grid=(m // bm, pl.cdiv(n_out, bn)), in_specs=in_specs,
        out_specs=pl.BlockSpec((bm, bn), lambda i, j: (i, j)),
        out_shape=jax.ShapeDtypeStruct((m, n_out), out_dtype),
        scratch_shapes=scratch, compiler_params=_params(2), name=name,
    )(*args)


def _ep_plain(accs):
    return accs[0]


def _ep_swiglu(accs):
    a, b = accs
    return (a * jax.nn.sigmoid(a)) * b


def _ep_residual(accs, x_ref, g_ref):
    return x_ref[...] + g_ref[...] * accs[0]


def _ep_merge(accs, yf_ref, yc_ref, ya_ref):
    g0, g1, g2 = (jax.nn.sigmoid(a) for a in accs)
    return g0 * yf_ref[...] + g1 * yc_ref[...] + g2 * ya_ref[...]


def _rot_half_pairs(x):
    lane = lax.broadcasted_iota(jnp.int32, x.shape, 1)
    first = (lane // 16) % 2 == 0
    return jnp.where(first, -pltpu.roll(x, LANES - 16, 1), pltpu.roll(x, 16, 1))


def _ep_rope_q(accs, cos_ref, sin_ref):
    acc = accs[0]
    cos, sin = cos_ref[...], sin_ref[...]
    parts = []
    for h in range(acc.shape[1] // HEAD_PAD):
        base = h * HEAD_PAD
        parts.append(acc[:, base:base + QK_NOPE])
        pe = acc[:, base + QK_NOPE:base + HEAD_PAD]
        parts.append(pe * cos + _rot_half_pairs(pe) * sin)
    return jnp.concatenate(parts, axis=1)


def _pro_rmsnorm(x_ref, g_ref):
    return (_rms(x_ref[...]) * g_ref[...]).astype(BF16)


def _dft_tables(n, sign):
    p = jnp.arange(n, dtype=jnp.int32)
    ang = ((p[:, None] * p[None, :]) % n).astype(F32) * np.float32(2.0 * np.pi / n)
    scale = np.float32(1.0 / np.sqrt(n))
    return _split_bf16(jnp.cos(ang) * scale) + _split_bf16(jnp.sin(ang) * (sign * scale))


def _fourier_chan_body(u_ref, c_hi, c_lo, s_hi, s_lo, a_hi, a_lo, b_hi, b_lo):
    for g in range(N_FOURIER_GROUPS):
        sl = slice(g * FOURIER_GROUP, (g + 1) * FOURIER_GROUP)
        u_h, u_l = _split_bf16(u_ref[:, sl])
        ah, al = _split_bf16(_dot3(u_h, u_l, c_hi[...], c_lo[...]))
        bh, bl = _split_bf16(_dot3(u_h, u_l, s_hi[...], s_lo[...]))
        a_hi[:, sl] = ah
        a_lo[:, sl] = al
        b_hi[:, sl] = bh
        b_lo[:, sl] = bl


def _fourier_chan(z, tabs, bm=512):
    m = z.shape[0]
    tab = pl.BlockSpec((FOURIER_GROUP, FOURIER_GROUP), lambda i: (0, 0))
    row = pl.BlockSpec((bm, D_FOURIER), lambda i: (i, 0))
    return pl.pallas_call(
        _fourier_chan_body, grid=(m // bm,),
        in_specs=[row, tab, tab, tab, tab], out_specs=[row] * 4,
        out_shape=[jax.ShapeDtypeStruct((m, D_FOURIER), BF16)] * 4,
        compiler_params=_params(1), name="fourier_chan",
    )(z, *tabs)


def _fourier_seq_body(c_hi, c_lo, s_hi, s_lo, a_hi, a_lo, b_hi, b_lo, o_ref):
    y = _dot3(c_hi[...], c_lo[...], a_hi[...], a_lo[...]) + _dot3(s_hi[...], s_lo[...], b_hi[...], b_lo[...])
    o_ref[...] = y.astype(o_ref.dtype)


def _fourier_seq(ab, tabs, n_seq, seq_len, seq_blk0, bl, bn):
    nr, nc = seq_len // bl, D_FOURIER // bn
    tab = pl.BlockSpec((bl, seq_len), lambda s, c, r: (r, 0))
    dat = pl.BlockSpec((seq_len, bn), lambda s, c, r: (seq_blk0 + s, c))
    return pl.pallas_call(
        _fourier_seq_body, grid=(n_seq, nc, nr),
        in_specs=[tab] * 4 + [dat] * 4,
        out_specs=pl.BlockSpec((bl, bn), lambda s, c, r: (s * nr + r, c)),
        out_shape=jax.ShapeDtypeStruct((n_seq * seq_len, D_FOURIER), BF16),
        compiler_params=_params(3), name=f"fourier_seq{seq_len}",
    )(*tabs, *ab)


CONV_ROWS = 2048


def _conv_body(b_ref, c_ref, h_ref, w_ref, o_ref):
    i = pl.program_id(0)
    seq_len = jnp.where(i * CONV_ROWS < N_PROMPT, SEQ, DEC_SEQ)
    u = c_ref[...] * h_ref[...]
    pos = lax.broadcasted_iota(jnp.int32, u.shape, 0) & (seq_len - 1)
    prev = jnp.where(pos == 0, 0.0, pltpu.roll(u, 1, 0))
    nxt = jnp.where(pos == seq_len - 1, 0.0, pltpu.roll(u, CONV_ROWS - 1, 0))
    w = w_ref[...]
    o_ref[...] = (b_ref[...] * (w[0:1] * prev + w[1:2] * u + w[2:3] * nxt)).astype(o_ref.dtype)


def _conv(z, conv_w, layer, bc=256):
    m = z.shape[0]
    nb = D_CONV // bc
    col0 = D_FOURIER // bc

    def part(k):
        return pl.BlockSpec((CONV_ROWS, bc), lambda i, j: (i, col0 + k * nb + j))

    return pl.pallas_call(
        _conv_body, grid=(m // CONV_ROWS, nb),
        in_specs=[part(0), part(1), part(2), pl.BlockSpec((None, 3, bc), lambda i, j: (layer, 0, j))],
        out_specs=pl.BlockSpec((CONV_ROWS, bc), lambda i, j: (i, j)),
        out_shape=jax.ShapeDtypeStruct((m, D_CONV), BF16),
        compiler_params=_params(2), name="short_conv",
    )(z, z, z, conv_w)


def _rope_k_body(x_ref, cos_ref, sin_ref, o_ref):
    x = x_ref[...]
    o_ref[...] = (x * cos_ref[...] + _rot_half_pairs(x) * sin_ref[...]).astype(o_ref.dtype)


def _rope_k(x, cos, sin, bm=512):
    m = x.shape[0]
    blk = pl.BlockSpec((bm, LANES), lambda i: (i, 0))
    return pl.pallas_call(
        _rope_k_body, grid=(m // bm,), in_specs=[blk, blk, blk], out_specs=blk,
        out_shape=jax.ShapeDtypeStruct((m, LANES), BF16),
        compiler_params=_params(1), name="rope_k",
    )(x, cos, sin)


def _attn_body(q_ref, kn_ref, kp_ref, v_ref, o_ref):
    q = q_ref[...]
    nt = (((1,), (1,)), ((), ()))
    s = lax.dot_general(q[:, :QK_NOPE], kn_ref[...], nt, preferred_element_type=F32)
    s = s + lax.dot_general(q[:, QK_NOPE:], kp_ref[...], nt, preferred_element_type=F32)
    s = s * np.float32((QK_NOPE + QK_ROPE) ** -0.5)
    p = jnp.exp(s - jnp.max(s, axis=-1, keepdims=True))
    l = jnp.sum(p, axis=-1, keepdims=True)
    o_ref[...] = (_dot(p.astype(BF16), v_ref[...]) / l).astype(o_ref.dtype)


def _attention(q, kv, kpe, n_seq, lq, lk, q_row0, k_row0, bq):
    nq = lq // bq
    return pl.pallas_call(
        _attn_body, grid=(n_seq, N_HEADS, nq),
        in_specs=[
            pl.BlockSpec((bq, HEAD_PAD), lambda b, h, i: (q_row0 // bq + b * nq + i, h)),
            pl.BlockSpec((lk, QK_NOPE), lambda b, h, i: (k_row0 // lk + b, 2 * h)),
            pl.BlockSpec((lk, LANES), lambda b, h, i: (k_row0 // lk + b, 0)),
            pl.BlockSpec((lk, V_HEAD), lambda b, h, i: (k_row0 // lk + b, 2 * h + 1)),
        ],
        out_specs=pl.BlockSpec((bq, V_HEAD), lambda b, h, i: (b * nq + i, h)),
        out_shape=jax.ShapeDtypeStruct((n_seq * lq, N_HEADS * V_HEAD), BF16),
        compiler_params=_params(3), name=f"attention{lq}",
    )(q, kv, kpe, kv)


def _route(idx):
    flat_e = idx.reshape(-1)
    n = flat_e.shape[0]
    onehot = (flat_e[:, None] == jnp.arange(N_EXPERTS, dtype=jnp.int32)[None, :]).astype(jnp.int32)
    csum = jnp.cumsum(onehot, axis=0)
    rank = jnp.take_along_axis(csum, flat_e[:, None], axis=1)[:, 0] - 1
    n_tile = (csum[-1] + MOE_BM - 1) // MOE_BM
    tile_end = jnp.cumsum(n_tile)
    pos = (tile_end - n_tile)[flat_e] * MOE_BM + rank
    n_valid = tile_end[-1]
    t = jnp.arange(MOE_TILES, dtype=jnp.int32)
    tile_e = jnp.searchsorted(tile_end, jnp.minimum(t, n_valid - 1), side="right").astype(jnp.int32)
    src = jnp.zeros((MOE_TILES * MOE_BM,), jnp.int32).at[pos].set(jnp.arange(n, dtype=jnp.int32) // TOP_K)
    return pos.astype(jnp.int32), src, tile_e, n_valid.reshape(1).astype(jnp.int32)


def _gather_body(src_ref, h_hbm, o_ref, buf, sem):
    base = pl.program_id(0) * GATHER_ROWS

    def row_copy(r, src_row):
        return pltpu.make_async_copy(h_hbm.at[pl.ds(src_row, 1)], buf.at[pl.ds(r, 1)], sem)

    def start(r, c):
        row_copy(r, src_ref[base + r]).start()
        return c

    def wait(r, c):
        row_copy(r, 0).wait()
        return c

    lax.fori_loop(0, GATHER_ROWS, start, 0)
    lax.fori_loop(0, GATHER_ROWS, wait, 0)
    o_ref[...] = buf[...].astype(o_ref.dtype)


def _gather_rows(h, src):
    n_slots = src.shape[0]
    grid_spec = pltpu.PrefetchScalarGridSpec(
        num_scalar_prefetch=1, grid=(n_slots // GATHER_ROWS,),
        in_specs=[pl.BlockSpec(memory_space=pl.ANY)],
        out_specs=pl.BlockSpec((GATHER_ROWS, D_MODEL), lambda i, s: (i, 0)),
        scratch_shapes=[pltpu.VMEM((GATHER_ROWS, D_MODEL), F32), pltpu.SemaphoreType.DMA(())],
    )
    return pl.pallas_call(
        _gather_body, grid_spec=grid_spec,
        out_shape=jax.ShapeDtypeStruct((n_slots, D_MODEL), BF16),
        compiler_params=_params(1), name="moe_gather",
    )(src, h)


def _moe_up_body(te_ref, nv_ref, x_ref, w1_ref, w3_ref, o_ref):
    valid = pl.program_id(0) < nv_ref[0]

    @pl.when(valid)
    def _():
        xv = x_ref[...]
        a = _dot(xv, w1_ref[...].astype(BF16))
        b = _dot(xv, w3_ref[...].astype(BF16))
        o_ref[...] = ((a * jax.nn.sigmoid(a)) * b).astype(o_ref.dtype)

    @pl.when(jnp.logical_not(valid))
    def _():
        o_ref[...] = jnp.zeros_like(o_ref)


def _moe_up(xs, w1, w3, moe_layer, tile_e, n_valid, bn=256):
    gj = D_FF_EXPERT // bn

    def row(t, nv):
        return jnp.minimum(t, nv[0] - 1)

    def col(t, j, nv):
        return jnp.where(t < nv[0], j, gj - 1)

    w_spec = pl.BlockSpec((None, None, D_MODEL, bn),
                          lambda t, j, te, nv: (moe_layer, te[t], 0, col(t, j, nv)))
    grid_spec = pltpu.PrefetchScalarGridSpec(
        num_scalar_prefetch=2, grid=(MOE_TILES, gj),
        in_specs=[pl.BlockSpec((MOE_BM, D_MODEL), lambda t, j, te, nv: (row(t, nv), 0)), w_spec, w_spec],
        out_specs=pl.BlockSpec((MOE_BM, bn), lambda t, j, te, nv: (t, j)),
    )
    return pl.pallas_call(
        _moe_up_body, grid_spec=grid_spec,
        out_shape=jax.ShapeDtypeStruct((xs.shape[0], D_FF_EXPERT), BF16),
        compiler_params=_params(2), name="moe_up",
    )(tile_e, n_valid, xs, w1, w3)


def _moe_down_body(te_ref, nv_ref, x_ref, w_ref, o_ref):
    valid = pl.program_id(0) < nv_ref[0]
    k = pl.program_id(2)

    @pl.when(valid & (k == 0))
    def _():
        o_ref[...] = _dot(x_ref[...], w_ref[...].astype(BF16))

    @pl.when(valid & (k > 0))
    def _():
        o_ref[...] += _dot(x_ref[...], w_ref[...].astype(BF16))

    @pl.when(jnp.logical_not(valid) & (k == 0))
    def _():
        o_ref[...] = jnp.zeros_like(o_ref)


def _moe_down(mid, w2, moe_layer, tile_e, n_valid, bn=2048, bk=1024):
    gj, gk = D_MODEL // bn, D_FF_EXPERT // bk

    def row(t, nv):
        return jnp.minimum(t, nv[0] - 1)

    def clamp(t, v, last, nv):
        return jnp.where(t < nv[0], v, last)

    grid_spec = pltpu.PrefetchScalarGridSpec(
        num_scalar_prefetch=2, grid=(MOE_TILES, gj, gk),
        in_specs=[
            pl.BlockSpec((MOE_BM, bk), lambda t, j, k, te, nv: (row(t, nv), clamp(t, k, gk - 1, nv))),
            pl.BlockSpec((None, None, bk, bn),
                         lambda t, j, k, te, nv: (moe_layer, te[t], clamp(t, k, gk - 1, nv),
                                                  clamp(t, j, gj - 1, nv))),
        ],
        out_specs=pl.BlockSpec((MOE_BM, bn), lambda t, j, k, te, nv: (t, j)),
    )
    return pl.pallas_call(
        _moe_down_body, grid_spec=grid_spec,
        out_shape=jax.ShapeDtypeStruct((mid.shape[0], D_MODEL), F32),
        compiler_params=_params(3), name="moe_down",
    )(tile_e, n_valid, mid, w2)


def _combine_body(pos_ref, y_hbm, wt_ref, x_ref, g_ref, o_ref, buf, sem):
    base = pl.program_id(0) * GATHER_ROWS

    def row_copy(r, c, src_row):
        return pltpu.make_async_copy(y_hbm.at[pl.ds(src_row, 1)], buf.at[c, pl.ds(r, 1)], sem)

    def start(r, carry):
        for c in range(TOP_K):
            row_copy(r, c, pos_ref[(base + r) * TOP_K + c]).start()
        return carry

    def wait(r, carry):
        for c in range(TOP_K):
            row_copy(r, c, 0).wait()
        return carry

    lax.fori_loop(0, GATHER_ROWS, start, 0)
    lax.fori_loop(0, GATHER_ROWS, wait, 0)
    wt = wt_ref[...]
    f = wt[:, 0:1] * buf[0] + wt[:, 1:2] * buf[1]
    o_ref[...] = x_ref[...] + g_ref[...] * f


def _moe_combine(y, pos, wts, x, mods, k_gate):
    m = x.shape[0]
    row = pl.BlockSpec((GATHER_ROWS, D_MODEL), lambda i, p: (i, 0))
    grid_spec = pltpu.PrefetchScalarGridSpec(
        num_scalar_prefetch=1, grid=(m // GATHER_ROWS,),
        in_specs=[
            pl.BlockSpec(memory_space=pl.ANY),
            pl.BlockSpec((GATHER_ROWS, LANES), lambda i, p: (i, 0)),
            row,
            pl.BlockSpec((None, 1, D_MODEL), lambda i, p: (_cond_of_tile(i, GATHER_ROWS) * 6 + k_gate, 0, 0)),
        ],
        out_specs=row,
        scratch_shapes=[pltpu.VMEM((TOP_K, GATHER_ROWS, D_MODEL), F32), pltpu.SemaphoreType.DMA(())],
    )
    return pl.pallas_call(
        _combine_body, grid_spec=grid_spec,
        out_shape=jax.ShapeDtypeStruct((m, D_MODEL), F32),
        compiler_params=_params(1), name="moe_combine",
    )(pos, y, wts, x, mods)


def _rope_tables():
    n_rows = DEC_SEQ // GRID_W
    row = jnp.repeat(jnp.arange(n_rows), GRID_W).astype(F32)
    col = jnp.tile(jnp.arange(GRID_W), n_rows).astype(F32)
    half = QK_ROPE // 2
    inv = ROPE_BASE ** (-(jnp.arange(half // 2, dtype=F32) * 2.0) / half)
    ang_r = row[:, None] * inv
    ang_c = col[:, None] * inv
    ang = jnp.concatenate([ang_r, ang_r, ang_c, ang_c], axis=-1)
    pad1 = jnp.ones((DEC_SEQ, LANES - QK_ROPE), F32)
    cos = jnp.concatenate([jnp.cos(ang), pad1], axis=1)
    sin = jnp.concatenate([jnp.sin(ang), 0 * pad1], axis=1)
    return cos, sin


def _mod_spec(bm, bn, which):
    return ((None, 1, bn), lambda i, j: (_cond_of_tile(i, bm) * 6 + which, 0, j))


def kernel(x_prompt, x_sample, cache_ckv, cache_kpe, c, c_ctx, norm1_g, norm2_g, w_ada, b_ada, w_in,
           w_fourier_out, conv_w, w_conv_out, q_norm_g, w_uq, kv_norm_g, w_ukv, w_mla_out, w_o,
           ffn_w1, ffn_w3, ffn_w2, router_w, router_b, moe_w1, moe_w3, moe_w2, final_g):
    bm = 1024
    x = jnp.concatenate([x_prompt.reshape(N_PROMPT, D_MODEL), x_sample.reshape(N_SAMPLE, D_MODEL)], axis=0)

    cond8 = jnp.concatenate([c_ctx[None, :], c, jnp.zeros((8 - 1 - DEC_BATCH, D_MODEL), F32)], axis=0)
    mods_all = _ada(cond8, w_ada, b_ada)[:, :1 + DEC_BATCH].reshape(DEPTH, (1 + DEC_BATCH) * 6, 1, D_MODEL)

    cos, sin = _rope_tables()
    one = jnp.ones((N_PROMPT, LANES), F32)
    cos_q = jnp.concatenate([one] + [cos] * DEC_BATCH, axis=0)
    sin_q = jnp.concatenate([0 * one] + [sin] * DEC_BATCH, axis=0)
    one_p = jnp.ones((PAST_LEN, LANES), F32)
    cos_k = jnp.concatenate([one_p, cos] * DEC_BATCH + [one], axis=0)
    sin_k = jnp.concatenate([0 * one_p, sin] * DEC_BATCH + [0 * one], axis=0)
    tab_chan = _dft_tables(FOURIER_GROUP, 1.0)
    tab_prompt = _dft_tables(SEQ, -1.0)
    tab_sample = _dft_tables(DEC_SEQ, -1.0)

    ckv_out, kpe_out = [], []
    for layer in range(DEPTH):
        mods = mods_all[layer]
        w_gate = w_in[layer, :, GATE_COL0:].astype(BF16)
        uq = w_uq[layer].reshape(Q_LORA, N_HEADS, QK_NOPE + QK_ROPE)
        uq = jnp.concatenate([uq, jnp.zeros((Q_LORA, N_HEADS, HEAD_PAD - QK_NOPE - QK_ROPE), F32)], axis=-1)
        uq = uq.reshape(Q_LORA, N_HEADS * HEAD_PAD).astype(BF16)

        h = _norm_mod(x, norm1_g, layer, mods, 1, 0)
        z = _mm("in_proj", h, [(w_in, (layer,), 0)], _ep_plain, bm=bm, bn=512, n_out=Z_MAIN, out_dtype=F32)

        ab = _fourier_chan(z, tab_chan)
        yf_pre = jnp.concatenate([
            _fourier_seq(ab, tab_prompt, BATCH, SEQ, 0, SEQ, D_FOURIER),
            _fourier_seq(ab, tab_sample, DEC_BATCH, DEC_SEQ, N_PROMPT // DEC_SEQ, 512, 512)], axis=0)
        y_f = _mm("fourier_out", yf_pre, [(w_fourier_out, (layer,), 0)], _ep_plain,
                  bm=bm, bn=512, n_out=D_MODEL, out_dtype=F32)

        yc_pre = _conv(z, conv_w, layer)
        y_c = _mm("conv_out", yc_pre, [(w_conv_out, (layer,), 0)], _ep_plain,
                  bm=bm, bn=512, n_out=D_MODEL, out_dtype=F32)

        q_col = (D_FOURIER + 3 * D_CONV) // Q_LORA
        q = _mm("q_proj", z, [(uq, (), 0)], _ep_rope_q, bm=bm, bn=512, n_out=N_HEADS * HEAD_PAD,
                out_dtype=BF16, k=Q_LORA, x_kblk=q_col, prologue=_pro_rmsnorm,
                pro_extras=[(q_norm_g.reshape(DEPTH, 1, Q_LORA), (None, 1, Q_LORA), lambda i, j: (layer, 0, 0))],
                extras=[(cos_q, (bm, LANES), lambda i, j: (i, 0)), (sin_q, (bm, LANES), lambda i, j: (i, 0))])
        kv_col = (D_FOURIER + 3 * D_CONV + Q_LORA) // KV_LORA
        c_kv = _rmsnorm(z, kv_norm_g.reshape(DEPTH, 1, KV_LORA), layer, KV_LORA, kv_col)
        k_pe = z[:, GATE_COL0 - QK_ROPE:GATE_COL0]
        ckv_out.append(c_kv[:N_PROMPT].reshape(BATCH, SEQ, KV_LORA))
        kpe_out.append(k_pe[:N_PROMPT].reshape(BATCH, SEQ, QK_ROPE))
        ckv_s = jnp.concatenate([cache_ckv[:, layer], c_kv[N_PROMPT:].reshape(DEC_BATCH, DEC_SEQ, KV_LORA)], axis=1)
        ckv_ext = jnp.concatenate([ckv_s.reshape(-1, KV_LORA), c_kv[:N_PROMPT]], axis=0).astype(BF16)
        kpe_s = jnp.concatenate([cache_kpe[:, layer], k_pe[N_PROMPT:].reshape(DEC_BATCH, DEC_SEQ, QK_ROPE)], axis=1)
        kpe_ext = jnp.concatenate([kpe_s.reshape(-1, QK_ROPE), k_pe[:N_PROMPT]], axis=0)
        kpe_ext = jnp.concatenate([kpe_ext, jnp.zeros((N_KV_ROWS, LANES - QK_ROPE), F32)], axis=1)
        kpe_ext = _rope_k(kpe_ext, cos_k, sin_k)
        kv = _mm("kv_proj", ckv_ext, [(w_ukv, (layer,), 0)], _ep_plain,
                 bm=512, bn=1024, n_out=N_HEADS * (QK_NOPE + V_HEAD), out_dtype=BF16)
        attn = jnp.concatenate([
            _attention(q, kv, kpe_ext, BATCH, SEQ, SEQ, 0, DEC_BATCH * KV_LEN, SEQ),
            _attention(q, kv, kpe_ext, DEC_BATCH, DEC_SEQ, KV_LEN, N_PROMPT, 0, 512)], axis=0)
        y_a = _mm("mla_out", attn, [(w_mla_out, (layer,), 0)], _ep_plain,
                  bm=bm, bn=512, n_out=D_MODEL, out_dtype=F32)

        nb = D_MODEL // 512
        tile = ((512, 512), lambda i, j: (i, j))
        merged = _mm("gate_merge", h, [(w_gate, (), b * nb) for b in range(N_BRANCH)], _ep_merge,
                     bm=512, bn=512, n_out=D_MODEL, out_dtype=BF16,
                     extras=[(y_f,) + tile, (y_c,) + tile, (y_a,) + tile])
        x = _mm("out_proj", merged, [(w_o, (layer,), 0)], _ep_residual, bm=bm, bn=512, n_out=D_MODEL,
                out_dtype=F32, extras=[(x, (bm, 512), lambda i, j: (i, j)), (mods,) + _mod_spec(bm, 512, 2)])

        j = layer // 2
        if layer % 2 == 0:
            h2 = _norm_mod(x, norm2_g, layer, mods, 4, 3)
            mid = _mm("ffn_up", h2, [(ffn_w1, (j,), 0), (ffn_w3, (j,), 0)], _ep_swiglu,
                      bm=bm, bn=256, n_out=D_FF, out_dtype=BF16)
            x = _mm("ffn_down", mid, [(ffn_w2[j].astype(BF16), (), 0)], _ep_residual, bm=512, bn=256,
                    n_out=D_MODEL, out_dtype=F32,
                    extras=[(x, (512, 256), lambda i, j: (i, j)), (mods,) + _mod_spec(512, 256, 5)])
        else:
            rw = jnp.concatenate([router_w[j], jnp.zeros((D_MODEL, LANES - N_EXPERTS), F32)], axis=1)
            rb = jnp.concatenate([router_b[j], jnp.zeros((LANES - N_EXPERTS,), F32)])[None, :]
            h2f, wts, idx = _norm_mod(x, norm2_g, layer, mods, 4, 3, router=_split_bf16(rw) + (rb,))
            pos, src, tile_e, n_valid = _route(idx[:, :TOP_K])
            xs = _gather_rows(h2f, src)
            mid = _moe_up(xs, moe_w1, moe_w3, j, tile_e, n_valid)
            y = _moe_down(mid, moe_w2, j, tile_e, n_valid)
            x = _moe_combine(y, pos, wts, x, mods, 5)

    y = _rmsnorm(x, final_g.reshape(1, 1, D_MODEL), 0, D_MODEL, 0, bm=256)
    return (y[:N_PROMPT].reshape(BATCH, SEQ, D_MODEL),
            y[N_PROMPT:].reshape(DEC_BATCH, DEC_SEQ, D_MODEL),
            jnp.stack(ckv_out, axis=1),
            jnp.stack(kpe_out, axis=1))
```

```python
import functools

import jax
import jax.numpy as jnp
import numpy as np
from jax import lax
from jax.experimental import pallas as pl
from jax.experimental.pallas import tpu as pltpu

F32 = jnp.float32
BF16 = jnp.bfloat16

D_MODEL = 4096
BATCH = 16
SEQ = 256
DEPTH = 2
DEC_BATCH = 2
DEC_SEQ = 2048
PAST_LEN = 256
GRID_W = 64
N_FOURIER_GROUPS = 4
FOURIER_GROUP = 256
D_FOURIER = N_FOURIER_GROUPS * FOURIER_GROUP
D_CONV = 1024
N_HEADS = 16
Q_LORA = 1024
KV_LORA = 512
QK_NOPE = 128
QK_ROPE = 64
V_HEAD = 128
ROPE_BASE = 10000.0
N_BRANCH = 3
D_FF = 11008
N_EXPERTS = 8
TOP_K = 2
D_FF_EXPERT = 14336
EPS = 1e-6

N_PROMPT = BATCH * SEQ
N_SAMPLE = DEC_BATCH * DEC_SEQ
N_TOK = N_PROMPT + N_SAMPLE
KV_LEN = PAST_LEN + DEC_SEQ
N_KV_ROWS = DEC_BATCH * KV_LEN + N_PROMPT
GATE_COL0 = D_FOURIER + 3 * D_CONV + Q_LORA + KV_LORA + QK_ROPE
Z_MAIN = 6144
HEAD_PAD = 256
LANES = 128

VMEM_LIMIT = 56 * 1024 * 1024
MOE_SUB = 256
MOE_BM = 5 * MOE_SUB
MOE_TILES = (TOP_K * N_TOK) // MOE_BM + N_EXPERTS
GATHER_ROWS = 256


def _params(n_axes, vmem=VMEM_LIMIT):
    return pltpu.CompilerParams(dimension_semantics=("arbitrary",) * n_axes, vmem_limit_bytes=vmem)


def _cond_of_tile(i, bm):
    r0 = i * bm
    return jnp.where(r0 < N_PROMPT, 0, 1 + (r0 - N_PROMPT) // DEC_SEQ)


def _split_bf16(x):
    hi = x.astype(BF16)
    lo = (x - hi.astype(F32)).astype(BF16)
    return hi, lo


def _dot(a, b):
    return jnp.dot(a, b, preferred_element_type=F32)


def _dot3(a_hi, a_lo, b_hi, b_lo):
    return _dot(a_hi, b_hi) + _dot(a_hi, b_lo) + _dot(a_lo, b_hi)


def _ada_body(c_ref, w_ref, b_ref, o_ref):
    c = c_ref[...]
    s = (c * jax.nn.sigmoid(c)).astype(BF16)
    o_ref[...] = _dot(s, w_ref[...].astype(BF16)) + b_ref[...]


def _ada(cond8, w_ada, b_ada, bn=512):
    n = w_ada.shape[-1]
    return pl.pallas_call(
        _ada_body,
        grid=(DEPTH, n // bn),
        in_specs=[
            pl.BlockSpec((8, D_MODEL), lambda l, j: (0, 0)),
            pl.BlockSpec((None, D_MODEL, bn), lambda l, j: (l, 0, j)),
            pl.BlockSpec((None, 1, bn), lambda l, j: (l, 0, j)),
        ],
        out_specs=pl.BlockSpec((None, 8, bn), lambda l, j: (l, 0, j)),
        out_shape=jax.ShapeDtypeStruct((DEPTH, 8, n), F32),
        compiler_params=_params(2),
        name="ada",
    )(cond8, w_ada, b_ada.reshape(DEPTH, 1, n))


def _rms(x):
    return x * lax.rsqrt(jnp.mean(x * x, axis=-1, keepdims=True) + EPS)


def _norm_mod_body(x_ref, g_ref, sc_ref, sh_ref, o_ref):
    y = _rms(x_ref[...]) * g_ref[...]
    o_ref[...] = (y * (1 + sc_ref[...]) + sh_ref[...]).astype(o_ref.dtype)


def _norm_mod_router_body(x_ref, g_ref, sc_ref, sh_ref, rw_hi_ref, rw_lo_ref, rb_ref,
                          of_ref, wt_ref, ix_ref):
    y = _rms(x_ref[...]) * g_ref[...]
    h = y * (1 + sc_ref[...]) + sh_ref[...]
    of_ref[...] = h
    h_hi, h_lo = _split_bf16(h)
    logits = _dot3(h_hi, h_lo, rw_hi_ref[...], rw_lo_ref[...]) + rb_ref[...]
    lane = lax.broadcasted_iota(jnp.int32, logits.shape, 1)
    lanef = lane.astype(F32)
    neg = jnp.float32(-jnp.inf)
    lg = jnp.where(lane < N_EXPERTS, logits, neg)
    m1 = jnp.max(lg, axis=-1, keepdims=True)
    i1 = jnp.min(jnp.where(lg == m1, lanef, float(LANES)), axis=-1, keepdims=True)
    lg2 = jnp.where(lanef == i1, neg, lg)
    m2 = jnp.max(lg2, axis=-1, keepdims=True)
    i2 = jnp.min(jnp.where(lg2 == m2, lanef, float(LANES)), axis=-1, keepdims=True)
    e = jnp.exp(m2 - m1)
    w1 = 1.0 / (1.0 + e)
    w2 = e / (1.0 + e)
    wt_ref[...] = jnp.where(lane == 0, w1, jnp.where(lane == 1, w2, 0.0))
    ix_ref[...] = jnp.where(lane == 0, i1, jnp.where(lane == 1, i2, 0.0)).astype(jnp.int32)


def _norm_mod(x, gain, layer, mods, k_scale, k_shift, router=None, bm=256):
    m = x.shape[0]
    in_specs = [
        pl.BlockSpec((bm, D_MODEL), lambda i: (i, 0)),
        pl.BlockSpec((None, 1, D_MODEL), lambda i: (layer, 0, 0)),
        pl.BlockSpec((None, 1, D_MODEL), lambda i: (_cond_of_tile(i, bm) * 6 + k_scale, 0, 0)),
        pl.BlockSpec((None, 1, D_MODEL), lambda i: (_cond_of_tile(i, bm) * 6 + k_shift, 0, 0)),
    ]
    args = [x, gain.reshape(DEPTH, 1, D_MODEL), mods, mods]
    row = pl.BlockSpec((bm, D_MODEL), lambda i: (i, 0))
    if router is None:
        return pl.pallas_call(
            _norm_mod_body, grid=(m // bm,), in_specs=in_specs, out_specs=row,
            out_shape=jax.ShapeDtypeStruct((m, D_MODEL), BF16),
            compiler_params=_params(1), name="norm_mod",
        )(*args)
    rw_hi, rw_lo, rb = router
    small = pl.BlockSpec((bm, LANES), lambda i: (i, 0))
    in_specs += [
        pl.BlockSpec((D_MODEL, LANES), lambda i: (0, 0)),
        pl.BlockSpec((D_MODEL, LANES), lambda i: (0, 0)),
        pl.BlockSpec((1, LANES), lambda i: (0, 0)),
    ]
    return pl.pallas_call(
        _norm_mod_router_body, grid=(m // bm,), in_specs=in_specs,
        out_specs=[row, small, small],
        out_shape=[jax.ShapeDtypeStruct((m, D_MODEL), F32),
                   jax.ShapeDtypeStruct((m, LANES), F32), jax.ShapeDtypeStruct((m, LANES), jnp.int32)],
        compiler_params=_params(1), name="norm_mod_router",
    )(*args, rw_hi, rw_lo, rb)


def _rmsnorm_body(x_ref, g_ref, o_ref):
    o_ref[...] = (_rms(x_ref[...]) * g_ref[...]).astype(o_ref.dtype)


def _rmsnorm(x, gain3, lead, width, col_blk, bm=512):
    m = x.shape[0]
    return pl.pallas_call(
        _rmsnorm_body, grid=(m // bm,),
        in_specs=[pl.BlockSpec((bm, width), lambda i: (i, col_blk)),
                  pl.BlockSpec((None, 1, width), lambda i: (lead, 0, 0))],
        out_specs=pl.BlockSpec((bm, width), lambda i: (i, 0)),
        out_shape=jax.ShapeDtypeStruct((m, width), F32),
        compiler_params=_params(1), name="rmsnorm",
    )(x, gain3)


def _dot_nt(a, b):
    return lax.dot_general(a, b, (((1,), (1,)), ((), ())), preferred_element_type=F32)


def _mm(name, x, ws, epilogue, *, bm, bn, n_out, out_dtype, k=None, x_kblk=0, extras=(),
        prologue=None, pro_extras=(), x_buffers=2):
    m = x.shape[0]
    k = k or x.shape[1]
    n_w, n_e, n_p = len(ws), len(extras), len(pro_extras)
    x_mode = {} if x_buffers == 2 else {"pipeline_mode": pl.Buffered(x_buffers)}
    in_specs = [pl.BlockSpec((bm, k), lambda i, j: (i, x_kblk), **x_mode)]
    args = [x]
    for arr, blk, imap in pro_extras:
        in_specs.append(pl.BlockSpec(blk, imap))
        args.append(arr)
    for arr, lead, col0, transposed in ws:
        squeeze = (None,) * len(lead)
        if transposed:
            in_specs.append(pl.BlockSpec((pl.Element(1),) * len(lead) + (pl.Element(bn), pl.Element(k)),
                                         lambda i, j, lead=lead, col0=col0:
                                         lead + (pl.multiple_of(col0 + j * bn, 8), 0)))
        else:
            assert col0 % bn == 0
            in_specs.append(pl.BlockSpec(squeeze + (k, bn),
                                         lambda i, j, lead=lead, col0=col0: lead + (0, j + col0 // bn)))
        args.append(arr)
    for arr, blk, imap in extras:
        in_specs.append(pl.BlockSpec(blk, imap))
        args.append(arr)

    def body(*refs):
        x_ref = refs[0]
        p_refs = refs[1:1 + n_p]
        w_refs = refs[1 + n_p:1 + n_p + n_w]
        e_refs = refs[1 + n_p + n_w:1 + n_p + n_w + n_e]
        o_ref = refs[1 + n_p + n_w + n_e]
        if prologue is None:
            xv = x_ref[...]
        else:
            xs_ref = refs[-1]

            @pl.when(pl.program_id(1) == 0)
            def _():
                xs_ref[...] = prologue(x_ref, *p_refs)

            xv = xs_ref[...]
        accs = []
        for (_, lead, _, transposed), w_ref in zip(ws, w_refs):
            if transposed:
                accs.append(_dot_nt(xv, w_ref[(0,) * len(lead)].astype(BF16)))
            else:
                accs.append(_dot(xv, w_ref[...].astype(BF16)))
        o_ref[...] = epilogue(accs, *e_refs).astype(o_ref.dtype)

    scratch = [] if prologue is None else [pltpu.VMEM((bm, k), BF16)]
    return pl.pallas_call(
        body, grid=(m // bm, pl.cdiv(n_out, bn)), in_specs=in_specs,
        out_specs=pl.BlockSpec((bm, bn), lambda i, j: (i, j)),
        out_shape=jax.ShapeDtypeStruct((m, n_out), out_dtype),
        scratch_shapes=scratch, compiler_params=_params(2), name=name,
    )(*args)


def _ep_plain(accs):
    return accs[0]


def _ep_swiglu(accs):
    a, b = accs
    return (a * jax.nn.sigmoid(a)) * b


def _ep_residual(accs, x_ref, g_ref):
    return x_ref[...] + g_ref[...] * accs[0]


def _ep_merge(accs, yf_ref, yc_ref, ya_ref):
    g0, g1, g2 = (jax.nn.sigmoid(a) for a in accs)
    return g0 * yf_ref[...] + g1 * yc_ref[...] + g2 * ya_ref[...]


def _rot_half_pairs(x):
    lane = lax.broadcasted_iota(jnp.int32, x.shape, 1)
    first = (lane // 16) % 2 == 0
    return jnp.where(first, -pltpu.roll(x, LANES - 16, 1), pltpu.roll(x, 16, 1))


def _ep_rope_q(accs, cos_ref, sin_ref):
    acc = accs[0]
    cos, sin = cos_ref[...], sin_ref[...]
    parts = []
    for h in range(acc.shape[1] // HEAD_PAD):
        base = h * HEAD_PAD
        parts.append(acc[:, base:base + QK_NOPE])
        pe = acc[:, base + QK_NOPE:base + HEAD_PAD]
        parts.append(pe * cos + _rot_half_pairs(pe) * sin)
    return jnp.concatenate(parts, axis=1)


def _pro_rmsnorm(x_ref, g_ref):
    return (_rms(x_ref[...]) * g_ref[...]).astype(BF16)


def _dft_tables(n, sign):
    p = jnp.arange(n, dtype=jnp.int32)
    ang = ((p[:, None] * p[None, :]) % n).astype(F32) * np.float32(2.0 * np.pi / n)
    scale = np.float32(1.0 / np.sqrt(n))
    return _split_bf16(jnp.cos(ang) * scale) + _split_bf16(jnp.sin(ang) * (sign * scale))


def _fourier_chan_body(u_ref, c_hi, c_lo, s_hi, s_lo, a_hi, a_lo, b_hi, b_lo):
    for g in range(N_FOURIER_GROUPS):
        sl = slice(g * FOURIER_GROUP, (g + 1) * FOURIER_GROUP)
        u_h, u_l = _split_bf16(u_ref[:, sl])
        ah, al = _split_bf16(_dot3(u_h, u_l, c_hi[...], c_lo[...]))
        bh, bl = _split_bf16(_dot3(u_h, u_l, s_hi[...], s_lo[...]))
        a_hi[:, sl] = ah
        a_lo[:, sl] = al
        b_hi[:, sl] = bh
        b_lo[:, sl] = bl


def _fourier_chan(z, tabs, bm=512):
    m = z.shape[0]
    tab = pl.BlockSpec((FOURIER_GROUP, FOURIER_GROUP), lambda i: (0, 0))
    row = pl.BlockSpec((bm, D_FOURIER), lambda i: (i, 0))
    return pl.pallas_call(
        _fourier_chan_body, grid=(m // bm,),
        in_specs=[row, tab, tab, tab, tab], out_specs=[row] * 4,
        out_shape=[jax.ShapeDtypeStruct((m, D_FOURIER), BF16)] * 4,
        compiler_params=_params(1), name="fourier_chan",
    )(z, *tabs)


def _fourier_seq_body(c_hi, c_lo, s_hi, s_lo, a_hi, a_lo, b_hi, b_lo, o_ref):
    y = _dot3(c_hi[...], c_lo[...], a_hi[...], a_lo[...]) + _dot3(s_hi[...], s_lo[...], b_hi[...], b_lo[...])
    o_ref[...] = y.astype(o_ref.dtype)


def _fourier_seq(ab, tabs, n_seq, seq_len, seq_blk0, bl, bn):
    nr, nc = seq_len // bl, D_FOURIER // bn
    tab = pl.BlockSpec((bl, seq_len), lambda s, c, r: (r, 0))
    dat = pl.BlockSpec((seq_len, bn), lambda s, c, r: (seq_blk0 + s, c))
    return pl.pallas_call(
        _fourier_seq_body, grid=(n_seq, nc, nr),
        in_specs=[tab] * 4 + [dat] * 4,
        out_specs=pl.BlockSpec((bl, bn), lambda s, c, r: (s * nr + r, c)),
        out_shape=jax.ShapeDtypeStruct((n_seq * seq_len, D_FOURIER), BF16),
        compiler_params=_params(3), name=f"fourier_seq{seq_len}",
    )(*tabs, *ab)


CONV_ROWS = 2048


def _conv_body(b_ref, c_ref, h_ref, w_ref, o_ref):
    i = pl.program_id(0)
    seq_len = jnp.where(i * CONV_ROWS < N_PROMPT, SEQ, DEC_SEQ)
    u = c_ref[...] * h_ref[...]
    pos = lax.broadcasted_iota(jnp.int32, u.shape, 0) & (seq_len - 1)
    prev = jnp.where(pos == 0, 0.0, pltpu.roll(u, 1, 0))
    nxt = jnp.where(pos == seq_len - 1, 0.0, pltpu.roll(u, CONV_ROWS - 1, 0))
    w = w_ref[...]
    o_ref[...] = (b_ref[...] * (w[0:1] * prev + w[1:2] * u + w[2:3] * nxt)).astype(o_ref.dtype)


def _conv(z, conv_w, layer, bc=256):
    m = z.shape[0]
    nb = D_CONV // bc
    col0 = D_FOURIER // bc

    def part(k):
        return pl.BlockSpec((CONV_ROWS, bc), lambda i, j: (i, col0 + k * nb + j))

    return pl.pallas_call(
        _conv_body, grid=(m // CONV_ROWS, nb),
        in_specs=[part(0), part(1), part(2), pl.BlockSpec((None, 3, bc), lambda i, j: (layer, 0, j))],
        out_specs=pl.BlockSpec((CONV_ROWS, bc), lambda i, j: (i, j)),
        out_shape=jax.ShapeDtypeStruct((m, D_CONV), BF16),
        compiler_params=_params(2), name="short_conv",
    )(z, z, z, conv_w)


def _rope_k_body(x_ref, cos_ref, sin_ref, o_ref):
    x = x_ref[...]
    o_ref[...] = (x * cos_ref[...] + _rot_half_pairs(x) * sin_ref[...]).astype(o_ref.dtype)


def _rope_k(x, cos, sin, bm=512):
    m = x.shape[0]
    blk = pl.BlockSpec((bm, LANES), lambda i: (i, 0))
    return pl.pallas_call(
        _rope_k_body, grid=(m // bm,), in_specs=[blk, blk, blk], out_specs=blk,
        out_shape=jax.ShapeDtypeStruct((m, LANES), BF16),
        compiler_params=_params(1), name="rope_k",
    )(x, cos, sin)


def _attn_body(q_ref, kv_ref, kp_ref, o_ref, *, heads):
    kp = kp_ref[...]
    for h in range(heads):
        q = q_ref[:, h * HEAD_PAD:(h + 1) * HEAD_PAD]
        kn = kv_ref[:, h * HEAD_PAD:h * HEAD_PAD + QK_NOPE]
        v = kv_ref[:, h * HEAD_PAD + QK_NOPE:(h + 1) * HEAD_PAD]
        s = _dot_nt(q[:, :QK_NOPE], kn) + _dot_nt(q[:, QK_NOPE:], kp)
        s = s * np.float32((QK_NOPE + QK_ROPE) ** -0.5)
        p = jnp.exp(s - jnp.max(s, axis=-1, keepdims=True))
        l = jnp.sum(p, axis=-1, keepdims=True)
        o_ref[:, h * V_HEAD:(h + 1) * V_HEAD] = (_dot(p.astype(BF16), v) / l).astype(o_ref.dtype)


def _attention(q, kv, kpe, n_seq, lq, lk, q_row0, k_row0, bq, heads):
    nq = lq // bq
    return pl.pallas_call(
        functools.partial(_attn_body, heads=heads), grid=(n_seq, N_HEADS // heads, nq),
        in_specs=[
            pl.BlockSpec((bq, heads * HEAD_PAD), lambda b, h, i: (q_row0 // bq + b * nq + i, h)),
            pl.BlockSpec((lk, heads * (QK_NOPE + V_HEAD)), lambda b, h, i: (k_row0 // lk + b, h)),
            pl.BlockSpec((lk, LANES), lambda b, h, i: (k_row0 // lk + b, 0)),
        ],
        out_specs=pl.BlockSpec((bq, heads * V_HEAD), lambda b, h, i: (b * nq + i, h)),
        out_shape=jax.ShapeDtypeStruct((n_seq * lq, N_HEADS * V_HEAD), BF16),
        compiler_params=_params(3), name=f"attention{lq}",
    )(q, kv, kpe)


def _route(idx):
    flat_e = idx.reshape(-1)
    n = flat_e.shape[0]
    onehot = (flat_e[:, None] == jnp.arange(N_EXPERTS, dtype=jnp.int32)[None, :]).astype(jnp.int32)
    csum = jnp.cumsum(onehot, axis=0)
    rank = jnp.take_along_axis(csum, flat_e[:, None], axis=1)[:, 0] - 1
    counts = csum[-1]
    n_tile = (counts + MOE_BM - 1) // MOE_BM
    tile_end = jnp.cumsum(n_tile)
    tile_start = tile_end - n_tile
    pos = tile_start[flat_e] * MOE_BM + rank
    n_valid = tile_end[-1]
    t = jnp.arange(MOE_TILES, dtype=jnp.int32)
    tc = jnp.minimum(t, n_valid - 1)
    tile_e = jnp.sum((tile_end[None, :] <= tc[:, None]).astype(jnp.int32), axis=1)
    rows = jnp.clip(counts[tile_e] - (tc - tile_start[tile_e]) * MOE_BM, 0, MOE_BM)
    tile_rows = jnp.where(t < n_valid, rows, 0)
    src = jnp.zeros((MOE_TILES * MOE_BM,), jnp.int32).at[pos].set(jnp.arange(n, dtype=jnp.int32) // TOP_K)
    i32 = jnp.int32
    return pos.astype(i32), src, tile_e.astype(i32), tile_rows.astype(i32), n_valid.reshape(1).astype(i32)


GATHER_UNROLL = 8


def _gather_body(src_ref, nr_ref, h_hbm, o_ref, buf, sem):
    t, s = pl.program_id(0), pl.program_id(1)
    base = t * MOE_BM + s * MOE_SUB
    active = s * MOE_SUB < nr_ref[t]

    def row_copy(r, src_row):
        return pltpu.make_async_copy(h_hbm.at[pl.ds(src_row, 1)], buf.at[pl.ds(r, 1)], sem)

    @pl.when(active)
    def _():
        def start(g, c):
            for u in range(GATHER_UNROLL):
                r = g * GATHER_UNROLL + u
                row_copy(r, src_ref[base + r]).start()
            return c

        def wait(g, c):
            for u in range(GATHER_UNROLL):
                row_copy(g * GATHER_UNROLL + u, 0).wait()
            return c

        lax.fori_loop(0, MOE_SUB // GATHER_UNROLL, start, 0)
        lax.fori_loop(0, MOE_SUB // GATHER_UNROLL, wait, 0)
        o_ref[...] = buf[...].astype(o_ref.dtype)

    @pl.when(jnp.logical_not(active))
    def _():
        o_ref[...] = jnp.zeros_like(o_ref)


def _gather_rows(h, src, tile_rows):
    n_sub = MOE_BM // MOE_SUB
    grid_spec = pltpu.PrefetchScalarGridSpec(
        num_scalar_prefetch=2, grid=(MOE_TILES, n_sub),
        in_specs=[pl.BlockSpec(memory_space=pl.ANY)],
        out_specs=pl.BlockSpec((MOE_SUB, D_MODEL), lambda t, s, src, nr: (t * n_sub + s, 0)),
        scratch_shapes=[pltpu.VMEM((MOE_SUB, D_MODEL), F32), pltpu.SemaphoreType.DMA(())],
    )
    return pl.pallas_call(
        _gather_body, grid_spec=grid_spec,
        out_shape=jax.ShapeDtypeStruct((MOE_TILES * MOE_BM, D_MODEL), BF16),
        compiler_params=_params(2), name="moe_gather",
    )(src, tile_rows, h)


def _sub_blocks(nr_ref):
    return (nr_ref[pl.program_id(0)] + MOE_SUB - 1) // MOE_SUB


def _moe_up_body(te_ref, nr_ref, nv_ref, x_ref, w1_ref, w3_ref, o_ref):
    n_sub = _sub_blocks(nr_ref)
    for n in range(MOE_BM // MOE_SUB + 1):
        @pl.when(n_sub == n)
        def _(n=n):
            m = n * MOE_SUB
            if m:
                xv = x_ref[:m, :]
                a = _dot(xv, w1_ref[...].astype(BF16))
                b = _dot(xv, w3_ref[...].astype(BF16))
                o_ref[:m, :] = ((a * jax.nn.sigmoid(a)) * b).astype(o_ref.dtype)
            if m < MOE_BM:
                o_ref[m:, :] = jnp.zeros((MOE_BM - m, o_ref.shape[1]), o_ref.dtype)


def _moe_up(xs, w1, w3, moe_layer, tile_e, tile_rows, n_valid, bn=256):
    gj = D_FF_EXPERT // bn

    def row(t, nv):
        return jnp.minimum(t, nv[0] - 1)

    def col(t, j, nv):
        return jnp.where(t < nv[0], j, gj - 1)

    w_spec = pl.BlockSpec((None, None, D_MODEL, bn),
                          lambda t, j, te, nr, nv: (moe_layer, te[t], 0, col(t, j, nv)))
    grid_spec = pltpu.PrefetchScalarGridSpec(
        num_scalar_prefetch=3, grid=(MOE_TILES, gj),
        in_specs=[pl.BlockSpec((MOE_BM, D_MODEL), lambda t, j, te, nr, nv: (row(t, nv), 0)), w_spec, w_spec],
        out_specs=pl.BlockSpec((MOE_BM, bn), lambda t, j, te, nr, nv: (t, j)),
    )
    return pl.pallas_call(
        _moe_up_body, grid_spec=grid_spec,
        out_shape=jax.ShapeDtypeStruct((xs.shape[0], D_FF_EXPERT), BF16),
        compiler_params=_params(2), name="moe_up",
    )(tile_e, tile_rows, n_valid, xs, w1, w3)


def _moe_down_body(te_ref, nr_ref, nv_ref, x_ref, w_ref, o_ref):
    n_sub = _sub_blocks(nr_ref)
    first = pl.program_id(2) == 0
    for n in range(MOE_BM // MOE_SUB + 1):
        m = n * MOE_SUB

        @pl.when((n_sub == n) & first)
        def _(m=m):
            if m:
                o_ref[:m, :] = _dot(x_ref[:m, :], w_ref[...].astype(BF16))
            if m < MOE_BM:
                o_ref[m:, :] = jnp.zeros((MOE_BM - m, o_ref.shape[1]), o_ref.dtype)

        if m:
            @pl.when((n_sub == n) & jnp.logical_not(first))
            def _(m=m):
                o_ref[:m, :] += _dot(x_ref[:m, :], w_ref[...].astype(BF16))


def _moe_down(mid, w2, moe_layer, tile_e, tile_rows, n_valid, bn=1024, bk=1024):
    gj, gk = D_MODEL // bn, D_FF_EXPERT // bk

    def row(t, nv):
        return jnp.minimum(t, nv[0] - 1)

    def clamp(t, v, last, nv):
        return jnp.where(t < nv[0], v, last)

    grid_spec = pltpu.PrefetchScalarGridSpec(
        num_scalar_prefetch=3, grid=(MOE_TILES, gj, gk),
        in_specs=[
            pl.BlockSpec((MOE_BM, bk), lambda t, j, k, te, nr, nv: (row(t, nv), clamp(t, k, gk - 1, nv))),
            pl.BlockSpec((None, None, bk, bn),
                         lambda t, j, k, te, nr, nv: (moe_layer, te[t], clamp(t, k, gk - 1, nv),
                                                      clamp(t, j, gj - 1, nv))),
        ],
        out_specs=pl.BlockSpec((MOE_BM, bn), lambda t, j, k, te, nr, nv: (t, j)),
    )
    return pl.pallas_call(
        _moe_down_body, grid_spec=grid_spec,
        out_shape=jax.ShapeDtypeStruct((mid.shape[0], D_MODEL), F32),
        compiler_params=_params(3), name="moe_down",
    )(tile_e, tile_rows, n_valid, mid, w2)


def _combine_body(pos_ref, y_hbm, wt_ref, x_ref, g_ref, o_ref, buf, sem):
    base = pl.program_id(0) * GATHER_ROWS

    def row_copy(r, c, src_row):
        return pltpu.make_async_copy(y_hbm.at[pl.ds(src_row, 1)], buf.at[c, pl.ds(r, 1)], sem)

    def start(g, carry):
        for u in range(GATHER_UNROLL):
            r = g * GATHER_UNROLL + u
            for c in range(TOP_K):
                row_copy(r, c, pos_ref[(base + r) * TOP_K + c]).start()
        return carry

    def wait(g, carry):
        for u in range(GATHER_UNROLL):
            for c in range(TOP_K):
                row_copy(g * GATHER_UNROLL + u, c, 0).wait()
        return carry

    lax.fori_loop(0, GATHER_ROWS // GATHER_UNROLL, start, 0)
    lax.fori_loop(0, GATHER_ROWS // GATHER_UNROLL, wait, 0)
    wt = wt_ref[...]
    f = wt[:, 0:1] * buf[0] + wt[:, 1:2] * buf[1]
    o_ref[...] = x_ref[...] + g_ref[...] * f


def _moe_combine(y, pos, wts, x, mods, k_gate):
    m = x.shape[0]
    row = pl.BlockSpec((GATHER_ROWS, D_MODEL), lambda i, p: (i, 0))
    grid_spec = pltpu.PrefetchScalarGridSpec(
        num_scalar_prefetch=1, grid=(m // GATHER_ROWS,),
        in_specs=[
            pl.BlockSpec(memory_space=pl.ANY),
            pl.BlockSpec((GATHER_ROWS, LANES), lambda i, p: (i, 0)),
            row,
            pl.BlockSpec((None, 1, D_MODEL), lambda i, p: (_cond_of_tile(i, GATHER_ROWS) * 6 + k_gate, 0, 0)),
        ],
        out_specs=row,
        scratch_shapes=[pltpu.VMEM((TOP_K, GATHER_ROWS, D_MODEL), F32), pltpu.SemaphoreType.DMA(())],
    )
    return pl.pallas_call(
        _combine_body, grid_spec=grid_spec,
        out_shape=jax.ShapeDtypeStruct((m, D_MODEL), F32),
        compiler_params=_params(1), name="moe_combine",
    )(pos, y, wts, x, mods)


def _rope_tables():
    n_rows = DEC_SEQ // GRID_W
    row = jnp.repeat(jnp.arange(n_rows), GRID_W).astype(F32)
    col = jnp.tile(jnp.arange(GRID_W), n_rows).astype(F32)
    half = QK_ROPE // 2
    inv = ROPE_BASE ** (-(jnp.arange(half // 2, dtype=F32) * 2.0) / half)
    ang_r = row[:, None] * inv
    ang_c = col[:, None] * inv
    ang = jnp.concatenate([ang_r, ang_r, ang_c, ang_c], axis=-1)
    pad1 = jnp.ones((DEC_SEQ, LANES - QK_ROPE), F32)
    cos = jnp.concatenate([jnp.cos(ang), pad1], axis=1)
    sin = jnp.concatenate([jnp.sin(ang), 0 * pad1], axis=1)
    return cos, sin


def _mod_spec(bm, bn, which):
    return ((None, 1, bn), lambda i, j: (_cond_of_tile(i, bm) * 6 + which, 0, j))


def kernel(x_prompt, x_sample, cache_ckv, cache_kpe, c, c_ctx, norm1_g, norm2_g, w_ada, b_ada, w_in,
           w_fourier_out, conv_w, w_conv_out, q_norm_g, w_uq, kv_norm_g, w_ukv, w_mla_out, w_o,
           ffn_w1, ffn_w3, ffn_w2, router_w, router_b, moe_w1, moe_w3, moe_w2, final_g):
    bm = 1024
    x = jnp.concatenate([x_prompt.reshape(N_PROMPT, D_MODEL), x_sample.reshape(N_SAMPLE, D_MODEL)], axis=0)
    w_in_t = jnp.swapaxes(w_in, 1, 2)

    cond8 = jnp.concatenate([c_ctx[None, :], c, jnp.zeros((8 - 1 - DEC_BATCH, D_MODEL), F32)], axis=0)
    mods_all = _ada(cond8, w_ada, b_ada)[:, :1 + DEC_BATCH].reshape(DEPTH, (1 + DEC_BATCH) * 6, 1, D_MODEL)

    cos, sin = _rope_tables()
    one = jnp.ones((N_PROMPT, LANES), F32)
    cos_q = jnp.concatenate([one] + [cos] * DEC_BATCH, axis=0)
    sin_q = jnp.concatenate([0 * one] + [sin] * DEC_BATCH, axis=0)
    one_p = jnp.ones((PAST_LEN, LANES), F32)
    cos_k = jnp.concatenate([one_p, cos] * DEC_BATCH + [one], axis=0)
    sin_k = jnp.concatenate([0 * one_p, sin] * DEC_BATCH + [0 * one], axis=0)
    tab_chan = _dft_tables(FOURIER_GROUP, 1.0)
    tab_prompt = _dft_tables(SEQ, -1.0)
    tab_sample = _dft_tables(DEC_SEQ, -1.0)

    ckv_out, kpe_out = [], []
    for layer in range(DEPTH):
        mods = mods_all[layer]
        uq = w_uq[layer].reshape(Q_LORA, N_HEADS, QK_NOPE + QK_ROPE)
        uq = jnp.concatenate([uq, jnp.zeros((Q_LORA, N_HEADS, HEAD_PAD - QK_NOPE - QK_ROPE), F32)], axis=-1)
        uq = uq.reshape(Q_LORA, N_HEADS * HEAD_PAD).astype(BF16)

        h = _norm_mod(x, norm1_g, layer, mods, 1, 0)
        z = _mm("in_proj", h, [(w_in_t, (layer,), 0, True)], _ep_plain, bm=bm, bn=512, n_out=Z_MAIN, out_dtype=F32)

        ab = _fourier_chan(z, tab_chan)
        yf_pre = jnp.concatenate([
            _fourier_seq(ab, tab_prompt, BATCH, SEQ, 0, SEQ, D_FOURIER),
            _fourier_seq(ab, tab_sample, DEC_BATCH, DEC_SEQ, N_PROMPT // DEC_SEQ, 512, 512)], axis=0)
        y_f = _mm("fourier_out", yf_pre, [(w_fourier_out, (layer,), 0, False)], _ep_plain,
                  bm=bm, bn=512, n_out=D_MODEL, out_dtype=F32)

        yc_pre = _conv(z, conv_w, layer)
        y_c = _mm("conv_out", yc_pre, [(w_conv_out, (layer,), 0, False)], _ep_plain,
                  bm=bm, bn=512, n_out=D_MODEL, out_dtype=F32)

        q_col = (D_FOURIER + 3 * D_CONV) // Q_LORA
        q = _mm("q_proj", z, [(uq, (), 0, False)], _ep_rope_q, bm=bm, bn=512, n_out=N_HEADS * HEAD_PAD,
                out_dtype=BF16, k=Q_LORA, x_kblk=q_col, prologue=_pro_rmsnorm,
                pro_extras=[(q_norm_g.reshape(DEPTH, 1, Q_LORA), (None, 1, Q_LORA), lambda i, j: (layer, 0, 0))],
                extras=[(cos_q, (bm, LANES), lambda i, j: (i, 0)), (sin_q, (bm, LANES), lambda i, j: (i, 0))])
        kv_col = (D_FOURIER + 3 * D_CONV + Q_LORA) // KV_LORA
        c_kv = _rmsnorm(z, kv_norm_g.reshape(DEPTH, 1, KV_LORA), layer, KV_LORA, kv_col)
        k_pe = z[:, GATE_COL0 - QK_ROPE:GATE_COL0]
        ckv_out.append(c_kv[:N_PROMPT].reshape(BATCH, SEQ, KV_LORA))
        kpe_out.append(k_pe[:N_PROMPT].reshape(BATCH, SEQ, QK_ROPE))
        ckv_s = jnp.concatenate([cache_ckv[:, layer], c_kv[N_PROMPT:].reshape(DEC_BATCH, DEC_SEQ, KV_LORA)], axis=1)
        ckv_ext = jnp.concatenate([ckv_s.reshape(-1, KV_LORA), c_kv[:N_PROMPT]], axis=0).astype(BF16)
        kpe_s = jnp.concatenate([cache_kpe[:, layer], k_pe[N_PROMPT:].reshape(DEC_BATCH, DEC_SEQ, QK_ROPE)], axis=1)
        kpe_ext = jnp.concatenate([kpe_s.reshape(-1, QK_ROPE), k_pe[:N_PROMPT]], axis=0)
        kpe_ext = jnp.concatenate([kpe_ext, jnp.zeros((N_KV_ROWS, LANES - QK_ROPE), F32)], axis=1)
        kpe_ext = _rope_k(kpe_ext, cos_k, sin_k)
        kv = _mm("kv_proj", ckv_ext, [(w_ukv, (layer,), 0, False)], _ep_plain,
                 bm=512, bn=1024, n_out=N_HEADS * (QK_NOPE + V_HEAD), out_dtype=BF16)
        attn = jnp.concatenate([
            _attention(q, kv, kpe_ext, BATCH, SEQ, SEQ, 0, DEC_BATCH * KV_LEN, SEQ, N_HEADS),
            _attention(q, kv, kpe_ext, DEC_BATCH, DEC_SEQ, KV_LEN, N_PROMPT, 0, 512, 1)], axis=0)
        y_a = _mm("mla_out", attn, [(w_mla_out, (layer,), 0, False)], _ep_plain,
                  bm=bm, bn=512, n_out=D_MODEL, out_dtype=F32)

        tile = ((bm, 256), lambda i, j: (i, j))
        merged = _mm("gate_merge", h,
                     [(w_in_t, (layer,), GATE_COL0 + b * D_MODEL, True) for b in range(N_BRANCH)], _ep_merge,
                     bm=bm, bn=256, n_out=D_MODEL, out_dtype=BF16, x_buffers=1,
                     extras=[(y_f,) + tile, (y_c,) + tile, (y_a,) + tile])
        x = _mm("out_proj", merged, [(w_o, (layer,), 0, False)], _ep_residual, bm=bm, bn=512, n_out=D_MODEL,
                out_dtype=F32, extras=[(x, (bm, 512), lambda i, j: (i, j)), (mods,) + _mod_spec(bm, 512, 2)])

        j = layer // 2
        if layer % 2 == 0:
            h2 = _norm_mod(x, norm2_g, layer, mods, 4, 3)
            mid = _mm("ffn_up", h2, [(ffn_w1, (j,), 0, False), (ffn_w3, (j,), 0, False)], _ep_swiglu,
                      bm=bm, bn=256, n_out=D_FF, out_dtype=BF16)
            x = _mm("ffn_down", mid, [(ffn_w2[j].astype(BF16), (), 0, False)], _ep_residual, bm=512, bn=256,
                    n_out=D_MODEL, out_dtype=F32,
                    extras=[(x, (512, 256), lambda i, j: (i, j)), (mods,) + _mod_spec(512, 256, 5)])
        else:
            rw = jnp.concatenate([router_w[j], jnp.zeros((D_MODEL, LANES - N_EXPERTS), F32)], axis=1)
            rb = jnp.concatenate([router_b[j], jnp.zeros((LANES - N_EXPERTS,), F32)])[None, :]
            h2f, wts, idx = _norm_mod(x, norm2_g, layer, mods, 4, 3, router=_split_bf16(rw) + (rb,))
            pos, src, tile_e, tile_rows, n_valid = _route(idx[:, :TOP_K])
            xs = _gather_rows(h2f, src, tile_rows)
            mid = _moe_up(xs, moe_w1, moe_w3, j, tile_e, tile_rows, n_valid)
            y = _moe_down(mid, moe_w2, j, tile_e, tile_rows, n_valid)
            x = _moe_combine(y, pos, wts, x, mods, 5)

    y = _rmsnorm(x, final_g.reshape(1, 1, D_MODEL), 0, D_MODEL, 0, bm=256)
    return (y[:N_PROMPT].reshape(BATCH, SEQ, D_MODEL),
            y[N_PROMPT:].reshape(DEC_BATCH, DEC_SEQ, D_MODEL),
            jnp.stack(ckv_out, axis=1),
            jnp.stack(kpe_out, axis=1))
```

```python
import functools

import jax
import jax.numpy as jnp
import numpy as np
from jax import lax
from jax.experimental import pallas as pl
from jax.experimental.pallas import tpu as pltpu

F32 = jnp.float32
BF16 = jnp.bfloat16

D_MODEL = 4096
BATCH = 16
SEQ = 256
DEPTH = 2
DEC_BATCH = 2
DEC_SEQ = 2048
PAST_LEN = 256
GRID_W = 64
N_FOURIER_GROUPS = 4
FOURIER_GROUP = 256
D_FOURIER = N_FOURIER_GROUPS * FOURIER_GROUP
D_CONV = 1024
N_HEADS = 16
Q_LORA = 1024
KV_LORA = 512
QK_NOPE = 128
QK_ROPE = 64
V_HEAD = 128
ROPE_BASE = 10000.0
N_BRANCH = 3
D_FF = 11008
N_EXPERTS = 8
TOP_K = 2
D_FF_EXPERT = 14336
EPS = 1e-6

N_PROMPT = BATCH * SEQ
N_SAMPLE = DEC_BATCH * DEC_SEQ
N_TOK = N_PROMPT + N_SAMPLE
KV_LEN = PAST_LEN + DEC_SEQ
N_KV_ROWS = DEC_BATCH * KV_LEN + N_PROMPT
GATE_COL0 = D_FOURIER + 3 * D_CONV + Q_LORA + KV_LORA + QK_ROPE
Z_MAIN = 6144
HEAD_PAD = 256
LANES = 128

VMEM_LIMIT = 56 * 1024 * 1024
MOE_SUB = 256
MOE_BM = 5 * MOE_SUB
MOE_TILES = (TOP_K * N_TOK) // MOE_BM + N_EXPERTS
GATHER_ROWS = 256


def _params(n_axes, vmem=VMEM_LIMIT):
    return pltpu.CompilerParams(dimension_semantics=("arbitrary",) * n_axes, vmem_limit_bytes=vmem)


def _cond_of_tile(i, bm):
    r0 = i * bm
    return jnp.where(r0 < N_PROMPT, 0, 1 + (r0 - N_PROMPT) // DEC_SEQ)


def _split_bf16(x):
    hi = x.astype(BF16)
    lo = (x - hi.astype(F32)).astype(BF16)
    return hi, lo


def _dot(a, b):
    return jnp.dot(a, b, preferred_element_type=F32)


def _dot3(a_hi, a_lo, b_hi, b_lo):
    return _dot(a_hi, b_hi) + _dot(a_hi, b_lo) + _dot(a_lo, b_hi)


def _ada_body(c_ref, w_ref, b_ref, o_ref):
    c = c_ref[...]
    s = (c * jax.nn.sigmoid(c)).astype(BF16)
    o_ref[...] = _dot(s, w_ref[...].astype(BF16)) + b_ref[...]


def _ada(cond8, w_ada, b_ada, bn=512):
    n = w_ada.shape[-1]
    return pl.pallas_call(
        _ada_body,
        grid=(DEPTH, n // bn),
        in_specs=[
            pl.BlockSpec((8, D_MODEL), lambda l, j: (0, 0)),
            pl.BlockSpec((None, D_MODEL, bn), lambda l, j: (l, 0, j)),
            pl.BlockSpec((None, 1, bn), lambda l, j: (l, 0, j)),
        ],
        out_specs=pl.BlockSpec((None, 8, bn), lambda l, j: (l, 0, j)),
        out_shape=jax.ShapeDtypeStruct((DEPTH, 8, n), F32),
        compiler_params=_params(2),
        name="ada",
    )(cond8, w_ada, b_ada.reshape(DEPTH, 1, n))


def _rms(x):
    return x * lax.rsqrt(jnp.mean(x * x, axis=-1, keepdims=True) + EPS)


def _norm_mod_body(x_ref, g_ref, sc_ref, sh_ref, o_ref):
    y = _rms(x_ref[...]) * g_ref[...]
    o_ref[...] = (y * (1 + sc_ref[...]) + sh_ref[...]).astype(o_ref.dtype)


def _norm_mod_router_body(x_ref, g_ref, sc_ref, sh_ref, rw_hi_ref, rw_lo_ref, rb_ref,
                          of_ref, wt_ref, ix_ref):
    y = _rms(x_ref[...]) * g_ref[...]
    h = y * (1 + sc_ref[...]) + sh_ref[...]
    of_ref[...] = h
    h_hi, h_lo = _split_bf16(h)
    logits = _dot3(h_hi, h_lo, rw_hi_ref[...], rw_lo_ref[...]) + rb_ref[...]
    lane = lax.broadcasted_iota(jnp.int32, logits.shape, 1)
    lanef = lane.astype(F32)
    neg = jnp.float32(-jnp.inf)
    lg = jnp.where(lane < N_EXPERTS, logits, neg)
    m1 = jnp.max(lg, axis=-1, keepdims=True)
    i1 = jnp.min(jnp.where(lg == m1, lanef, float(LANES)), axis=-1, keepdims=True)
    lg2 = jnp.where(lanef == i1, neg, lg)
    m2 = jnp.max(lg2, axis=-1, keepdims=True)
    i2 = jnp.min(jnp.where(lg2 == m2, lanef, float(LANES)), axis=-1, keepdims=True)
    e = jnp.exp(m2 - m1)
    w1 = 1.0 / (1.0 + e)
    w2 = e / (1.0 + e)
    wt_ref[...] = jnp.where(lane == 0, w1, jnp.where(lane == 1, w2, 0.0))
    ix_ref[...] = jnp.where(lane == 0, i1, jnp.where(lane == 1, i2, 0.0)).astype(jnp.int32)


def _norm_mod(x, gain, layer, mods, k_scale, k_shift, router=None, bm=256):
    m = x.shape[0]
    in_specs = [
        pl.BlockSpec((bm, D_MODEL), lambda i: (i, 0)),
        pl.BlockSpec((None, 1, D_MODEL), lambda i: (layer, 0, 0)),
        pl.BlockSpec((None, 1, D_MODEL), lambda i: (_cond_of_tile(i, bm) * 6 + k_scale, 0, 0)),
        pl.BlockSpec((None, 1, D_MODEL), lambda i: (_cond_of_tile(i, bm) * 6 + k_shift, 0, 0)),
    ]
    args = [x, gain.reshape(DEPTH, 1, D_MODEL), mods, mods]
    row = pl.BlockSpec((bm, D_MODEL), lambda i: (i, 0))
    if router is None:
        return pl.pallas_call(
            _norm_mod_body, grid=(m // bm,), in_specs=in_specs, out_specs=row,
            out_shape=jax.ShapeDtypeStruct((m, D_MODEL), BF16),
            compiler_params=_params(1), name="norm_mod",
        )(*args)
    rw_hi, rw_lo, rb = router
    small = pl.BlockSpec((bm, LANES), lambda i: (i, 0))
    in_specs += [
        pl.BlockSpec((D_MODEL, LANES), lambda i: (0, 0)),
        pl.BlockSpec((D_MODEL, LANES), lambda i: (0, 0)),
        pl.BlockSpec((1, LANES), lambda i: (0, 0)),
    ]
    return pl.pallas_call(
        _norm_mod_router_body, grid=(m // bm,), in_specs=in_specs,
        out_specs=[row, small, small],
        out_shape=[jax.ShapeDtypeStruct((m, D_MODEL), F32),
                   jax.ShapeDtypeStruct((m, LANES), F32), jax.ShapeDtypeStruct((m, LANES), jnp.int32)],
        compiler_params=_params(1), name="norm_mod_router",
    )(*args, rw_hi, rw_lo, rb)


def _rmsnorm_body(x_ref, g_ref, o_ref):
    o_ref[...] = (_rms(x_ref[...]) * g_ref[...]).astype(o_ref.dtype)


def _rmsnorm(x, gain3, lead, width, col_blk, bm=512, row0=0, rows=None):
    m = rows or x.shape[0]
    return pl.pallas_call(
        _rmsnorm_body, grid=(m // bm,),
        in_specs=[pl.BlockSpec((bm, width), lambda i: (row0 // bm + i, col_blk)),
                  pl.BlockSpec((None, 1, width), lambda i: (lead, 0, 0))],
        out_specs=pl.BlockSpec((bm, width), lambda i: (i, 0)),
        out_shape=jax.ShapeDtypeStruct((m, width), F32),
        compiler_params=_params(1), name="rmsnorm",
    )(x, gain3)


def _dot_nt(a, b):
    return lax.dot_general(a, b, (((1,), (1,)), ((), ())), preferred_element_type=F32)


def _mm(name, x, ws, epilogue, *, bm, bn, n_out, out_dtype, k=None, x_kblk=0, extras=(),
        prologue=None, pro_extras=(), x_buffers=2):
    m = x.shape[0]
    k = k or x.shape[1]
    n_w, n_e, n_p = len(ws), len(extras), len(pro_extras)
    x_mode = {} if x_buffers == 2 else {"pipeline_mode": pl.Buffered(x_buffers)}
    in_specs = [pl.BlockSpec((bm, k), lambda i, j: (i, x_kblk), **x_mode)]
    args = [x]
    for arr, blk, imap in pro_extras:
        in_specs.append(pl.BlockSpec(blk, imap))
        args.append(arr)
    for arr, lead, col0, transposed in ws:
        squeeze = (None,) * len(lead)
        if transposed:
            in_specs.append(pl.BlockSpec((pl.Element(1),) * len(lead) + (pl.Element(bn), pl.Element(k)),
                                         lambda i, j, lead=lead, col0=col0:
                                         lead + (pl.multiple_of(col0 + j * bn, 8), 0)))
        else:
            assert col0 % bn == 0
            in_specs.append(pl.BlockSpec(squeeze + (k, bn),
                                         lambda i, j, lead=lead, col0=col0: lead + (0, j + col0 // bn)))
        args.append(arr)
    for arr, blk, imap in extras:
        in_specs.append(pl.BlockSpec(blk, imap))
        args.append(arr)

    def body(*refs):
        x_ref = refs[0]
        p_refs = refs[1:1 + n_p]
        w_refs = refs[1 + n_p:1 + n_p + n_w]
        e_refs = refs[1 + n_p + n_w:1 + n_p + n_w + n_e]
        o_ref = refs[1 + n_p + n_w + n_e]
        if prologue is None:
            xv = x_ref[...]
        else:
            xs_ref = refs[-1]

            @pl.when(pl.program_id(1) == 0)
            def _():
                xs_ref[...] = prologue(x_ref, *p_refs)

            xv = xs_ref[...]
        accs = []
        for (_, lead, _, transposed), w_ref in zip(ws, w_refs):
            if transposed:
                accs.append(_dot_nt(xv, w_ref[(0,) * len(lead)].astype(BF16)))
            else:
                accs.append(_dot(xv, w_ref[...].astype(BF16)))
        o_ref[...] = epilogue(accs, *e_refs).astype(o_ref.dtype)

    scratch = [] if prologue is None else [pltpu.VMEM((bm, k), BF16)]
    return pl.pallas_call(
        body, grid=(m // bm, pl.cdiv(n_out, bn)), in_specs=in_specs,
        out_specs=pl.BlockSpec((bm, bn), lambda i, j: (i, j)),
        out_shape=jax.ShapeDtypeStruct((m, n_out), out_dtype),
        scratch_shapes=scratch, compiler_params=_params(2), name=name,
    )(*args)


def _ep_plain(accs):
    return accs[0]


def _ep_swiglu(accs):
    a, b = accs
    return (a * jax.nn.sigmoid(a)) * b


def _ep_residual(accs, x_ref, g_ref):
    return x_ref[...] + g_ref[...] * accs[0]


def _ep_merge(accs, yf_ref, yc_ref, ya_ref):
    g0, g1, g2 = (jax.nn.sigmoid(a) for a in accs)
    return g0 * yf_ref[...] + g1 * yc_ref[...] + g2 * ya_ref[...]


def _rot_half_pairs(x):
    lane = lax.broadcasted_iota(jnp.int32, x.shape, 1)
    first = (lane // 16) % 2 == 0
    return jnp.where(first, -pltpu.roll(x, LANES - 16, 1), pltpu.roll(x, 16, 1))


def _ep_rope_q(accs, cos_ref, sin_ref):
    acc = accs[0]
    cos, sin = cos_ref[...], sin_ref[...]
    parts = []
    for h in range(acc.shape[1] // HEAD_PAD):
        base = h * HEAD_PAD
        parts.append(acc[:, base:base + QK_NOPE])
        pe = acc[:, base + QK_NOPE:base + HEAD_PAD]
        parts.append(pe * cos + _rot_half_pairs(pe) * sin)
    return jnp.concatenate(parts, axis=1)


def _pro_rmsnorm(x_ref, g_ref):
    return (_rms(x_ref[...]) * g_ref[...]).astype(BF16)


def _dft_tables(n, sign):
    p = jnp.arange(n, dtype=jnp.int32)
    ang = ((p[:, None] * p[None, :]) % n).astype(F32) * np.float32(2.0 * np.pi / n)
    scale = np.float32(1.0 / np.sqrt(n))
    return _split_bf16(jnp.cos(ang) * scale) + _split_bf16(jnp.sin(ang) * (sign * scale))


def _fourier_chan_body(u_ref, c_hi, c_lo, s_hi, s_lo, a_hi, a_lo, b_hi, b_lo):
    for g in range(N_FOURIER_GROUPS):
        sl = slice(g * FOURIER_GROUP, (g + 1) * FOURIER_GROUP)
        u_h, u_l = _split_bf16(u_ref[:, sl])
        ah, al = _split_bf16(_dot3(u_h, u_l, c_hi[...], c_lo[...]))
        bh, bl = _split_bf16(_dot3(u_h, u_l, s_hi[...], s_lo[...]))
        a_hi[:, sl] = ah
        a_lo[:, sl] = al
        b_hi[:, sl] = bh
        b_lo[:, sl] = bl


def _fourier_chan(z, tabs, bm=512):
    m = z.shape[0]
    tab = pl.BlockSpec((FOURIER_GROUP, FOURIER_GROUP), lambda i: (0, 0))
    row = pl.BlockSpec((bm, D_FOURIER), lambda i: (i, 0))
    return pl.pallas_call(
        _fourier_chan_body, grid=(m // bm,),
        in_specs=[row, tab, tab, tab, tab], out_specs=[row] * 4,
        out_shape=[jax.ShapeDtypeStruct((m, D_FOURIER), BF16)] * 4,
        compiler_params=_params(1), name="fourier_chan",
    )(z, *tabs)


def _fourier_seq_body(c_hi, c_lo, s_hi, s_lo, a_hi, a_lo, b_hi, b_lo, o_ref):
    y = _dot3(c_hi[...], c_lo[...], a_hi[...], a_lo[...]) + _dot3(s_hi[...], s_lo[...], b_hi[...], b_lo[...])
    o_ref[...] = y.astype(o_ref.dtype)


def _fourier_seq(ab, tabs, n_seq, seq_len, seq_blk0, bl, bn):
    nr, nc = seq_len // bl, D_FOURIER // bn
    tab = pl.BlockSpec((bl, seq_len), lambda s, c, r: (r, 0))
    dat = pl.BlockSpec((seq_len, bn), lambda s, c, r: (seq_blk0 + s, c))
    return pl.pallas_call(
        _fourier_seq_body, grid=(n_seq, nc, nr),
        in_specs=[tab] * 4 + [dat] * 4,
        out_specs=pl.BlockSpec((bl, bn), lambda s, c, r: (s * nr + r, c)),
        out_shape=jax.ShapeDtypeStruct((n_seq * seq_len, D_FOURIER), BF16),
        compiler_params=_params(3), name=f"fourier_seq{seq_len}",
    )(*tabs, *ab)


CONV_ROWS = 2048


def _conv_body(b_ref, c_ref, h_ref, w_ref, o_ref):
    i = pl.program_id(0)
    seq_len = jnp.where(i * CONV_ROWS < N_PROMPT, SEQ, DEC_SEQ)
    u = c_ref[...] * h_ref[...]
    pos = lax.broadcasted_iota(jnp.int32, u.shape, 0) & (seq_len - 1)
    prev = jnp.where(pos == 0, 0.0, pltpu.roll(u, 1, 0))
    nxt = jnp.where(pos == seq_len - 1, 0.0, pltpu.roll(u, CONV_ROWS - 1, 0))
    w = w_ref[...]
    o_ref[...] = (b_ref[...] * (w[0:1] * prev + w[1:2] * u + w[2:3] * nxt)).astype(o_ref.dtype)


def _conv(z, conv_w, layer, bc=256):
    m = z.shape[0]
    nb = D_CONV // bc
    col0 = D_FOURIER // bc

    def part(k):
        return pl.BlockSpec((CONV_ROWS, bc), lambda i, j: (i, col0 + k * nb + j))

    return pl.pallas_call(
        _conv_body, grid=(m // CONV_ROWS, nb),
        in_specs=[part(0), part(1), part(2), pl.BlockSpec((None, 3, bc), lambda i, j: (layer, 0, j))],
        out_specs=pl.BlockSpec((CONV_ROWS, bc), lambda i, j: (i, j)),
        out_shape=jax.ShapeDtypeStruct((m, D_CONV), BF16),
        compiler_params=_params(2), name="short_conv",
    )(z, z, z, conv_w)


def _rope_k_body(x_ref, cos_ref, sin_ref, o_ref):
    x = x_ref[...]
    o_ref[...] = (x * cos_ref[...] + _rot_half_pairs(x) * sin_ref[...]).astype(o_ref.dtype)


def _rope_k(x, cos, sin, bm=512):
    m = x.shape[0]
    blk = pl.BlockSpec((bm, LANES), lambda i: (i, 0))
    return pl.pallas_call(
        _rope_k_body, grid=(m // bm,), in_specs=[blk, blk, blk], out_specs=blk,
        out_shape=jax.ShapeDtypeStruct((m, LANES), BF16),
        compiler_params=_params(1), name="rope_k",
    )(x, cos, sin)


def _attn_body(q_ref, kv_ref, kp_ref, o_ref, kcat_ref, *, heads):
    @pl.when(pl.program_id(2) == 0)
    def _():
        kp = kp_ref[...]
        for h in range(heads):
            kcat_ref[h, :, :QK_NOPE] = kv_ref[:, h * HEAD_PAD:h * HEAD_PAD + QK_NOPE]
            kcat_ref[h, :, QK_NOPE:] = kp

    for h in range(heads):
        q = q_ref[:, h * HEAD_PAD:(h + 1) * HEAD_PAD]
        v = kv_ref[:, h * HEAD_PAD + QK_NOPE:(h + 1) * HEAD_PAD]
        s = _dot_nt(q, kcat_ref[h]) * np.float32((QK_NOPE + QK_ROPE) ** -0.5)
        p = jnp.exp(s - jnp.max(s, axis=-1, keepdims=True))
        l = jnp.sum(p, axis=-1, keepdims=True)
        o_ref[:, h * V_HEAD:(h + 1) * V_HEAD] = (_dot(p.astype(BF16), v) / l).astype(o_ref.dtype)


def _attention(q, kv, kpe, n_seq, lq, lk, q_row0, k_row0, bq, heads):
    nq = lq // bq
    return pl.pallas_call(
        functools.partial(_attn_body, heads=heads), grid=(n_seq, N_HEADS // heads, nq),
        in_specs=[
            pl.BlockSpec((bq, heads * HEAD_PAD), lambda b, h, i: (q_row0 // bq + b * nq + i, h)),
            pl.BlockSpec((lk, heads * (QK_NOPE + V_HEAD)), lambda b, h, i: (k_row0 // lk + b, h)),
            pl.BlockSpec((lk, LANES), lambda b, h, i: (k_row0 // lk + b, 0)),
        ],
        out_specs=pl.BlockSpec((bq, heads * V_HEAD), lambda b, h, i: (b * nq + i, h)),
        out_shape=jax.ShapeDtypeStruct((n_seq * lq, N_HEADS * V_HEAD), BF16),
        scratch_shapes=[pltpu.VMEM((heads, lk, HEAD_PAD), BF16)],
        compiler_params=_params(3), name=f"attention{lq}",
    )(q, kv, kpe)


def _route(idx):
    flat_e = idx.reshape(-1)
    n = flat_e.shape[0]
    onehot = (flat_e[:, None] == jnp.arange(N_EXPERTS, dtype=jnp.int32)[None, :]).astype(jnp.int32)
    csum = jnp.cumsum(onehot, axis=0)
    rank = jnp.take_along_axis(csum, flat_e[:, None], axis=1)[:, 0] - 1
    counts = csum[-1]
    n_tile = (counts + MOE_BM - 1) // MOE_BM
    tile_end = jnp.cumsum(n_tile)
    tile_start = tile_end - n_tile
    pos = tile_start[flat_e] * MOE_BM + rank
    n_valid = tile_end[-1]
    t = jnp.arange(MOE_TILES, dtype=jnp.int32)
    tc = jnp.minimum(t, n_valid - 1)
    tile_e = jnp.sum((tile_end[None, :] <= tc[:, None]).astype(jnp.int32), axis=1)
    rows = jnp.clip(counts[tile_e] - (tc - tile_start[tile_e]) * MOE_BM, 0, MOE_BM)
    tile_rows = jnp.where(t < n_valid, rows, 0)
    src = jnp.zeros((MOE_TILES * MOE_BM,), jnp.int32).at[pos].set(jnp.arange(n, dtype=jnp.int32) // TOP_K)
    i32 = jnp.int32
    return pos.astype(i32), src, tile_e.astype(i32), tile_rows.astype(i32), n_valid.reshape(1).astype(i32)


GATHER_UNROLL = 8


def _gather_body(src_ref, nr_ref, h_hbm, o_ref, buf, sem):
    t, s = pl.program_id(0), pl.program_id(1)
    base = t * MOE_BM + s * MOE_SUB
    active = s * MOE_SUB < nr_ref[t]

    def row_copy(r, src_row):
        return pltpu.make_async_copy(h_hbm.at[pl.ds(src_row, 1)], buf.at[pl.ds(r, 1)], sem)

    @pl.when(active)
    def _():
        def start(g, c):
            for u in range(GATHER_UNROLL):
                r = g * GATHER_UNROLL + u
                row_copy(r, src_ref[base + r]).start()
            return c

        def wait(g, c):
            for u in range(GATHER_UNROLL):
                row_copy(g * GATHER_UNROLL + u, 0).wait()
            return c

        lax.fori_loop(0, MOE_SUB // GATHER_UNROLL, start, 0)
        lax.fori_loop(0, MOE_SUB // GATHER_UNROLL, wait, 0)
        o_ref[...] = buf[...].astype(o_ref.dtype)

    @pl.when(jnp.logical_not(active))
    def _():
        o_ref[...] = jnp.zeros_like(o_ref)


def _gather_rows(h, src, tile_rows):
    n_sub = MOE_BM // MOE_SUB
    grid_spec = pltpu.PrefetchScalarGridSpec(
        num_scalar_prefetch=2, grid=(MOE_TILES, n_sub),
        in_specs=[pl.BlockSpec(memory_space=pl.ANY)],
        out_specs=pl.BlockSpec((MOE_SUB, D_MODEL), lambda t, s, src, nr: (t * n_sub + s, 0)),
        scratch_shapes=[pltpu.VMEM((MOE_SUB, D_MODEL), F32), pltpu.SemaphoreType.DMA(())],
    )
    return pl.pallas_call(
        _gather_body, grid_spec=grid_spec,
        out_shape=jax.ShapeDtypeStruct((MOE_TILES * MOE_BM, D_MODEL), BF16),
        compiler_params=_params(2), name="moe_gather",
    )(src, tile_rows, h)


def _sub_blocks(nr_ref):
    return (nr_ref[pl.program_id(0)] + MOE_SUB - 1) // MOE_SUB


def _moe_up_body(te_ref, nr_ref, nv_ref, x_ref, w1_ref, w3_ref, o_ref):
    n_sub = _sub_blocks(nr_ref)
    for n in range(MOE_BM // MOE_SUB + 1):
        @pl.when(n_sub == n)
        def _(n=n):
            m = n * MOE_SUB
            if m:
                xv = x_ref[:m, :]
                a = _dot(xv, w1_ref[...].astype(BF16))
                b = _dot(xv, w3_ref[...].astype(BF16))
                o_ref[:m, :] = ((a * jax.nn.sigmoid(a)) * b).astype(o_ref.dtype)
            if m < MOE_BM:
                o_ref[m:, :] = jnp.zeros((MOE_BM - m, o_ref.shape[1]), o_ref.dtype)


def _moe_up(xs, w1, w3, moe_layer, tile_e, tile_rows, n_valid, bn=256):
    gj = D_FF_EXPERT // bn

    def row(t, nv):
        return jnp.minimum(t, nv[0] - 1)

    def col(t, j, nv):
        return jnp.where(t < nv[0], j, gj - 1)

    w_spec = pl.BlockSpec((None, None, D_MODEL, bn),
                          lambda t, j, te, nr, nv: (moe_layer, te[t], 0, col(t, j, nv)))
    grid_spec = pltpu.PrefetchScalarGridSpec(
        num_scalar_prefetch=3, grid=(MOE_TILES, gj),
        in_specs=[pl.BlockSpec((MOE_BM, D_MODEL), lambda t, j, te, nr, nv: (row(t, nv), 0)), w_spec, w_spec],
        out_specs=pl.BlockSpec((MOE_BM, bn), lambda t, j, te, nr, nv: (t, j)),
    )
    return pl.pallas_call(
        _moe_up_body, grid_spec=grid_spec,
        out_shape=jax.ShapeDtypeStruct((xs.shape[0], D_FF_EXPERT), BF16),
        compiler_params=_params(2), name="moe_up",
    )(tile_e, tile_rows, n_valid, xs, w1, w3)


def _moe_down_body(te_ref, nr_ref, nv_ref, x_ref, w_ref, o_ref):
    n_sub = _sub_blocks(nr_ref)
    first = pl.program_id(2) == 0
    for n in range(MOE_BM // MOE_SUB + 1):
        m = n * MOE_SUB

        @pl.when((n_sub == n) & first)
        def _(m=m):
            if m:
                o_ref[:m, :] = _dot(x_ref[:m, :], w_ref[...].astype(BF16))
            if m < MOE_BM:
                o_ref[m:, :] = jnp.zeros((MOE_BM - m, o_ref.shape[1]), o_ref.dtype)

        if m:
            @pl.when((n_sub == n) & jnp.logical_not(first))
            def _(m=m):
                o_ref[:m, :] += _dot(x_ref[:m, :], w_ref[...].astype(BF16))


def _moe_down(mid, w2, moe_layer, tile_e, tile_rows, n_valid, bn=1024, bk=2048):
    gj, gk = D_MODEL // bn, D_FF_EXPERT // bk

    def row(t, nv):
        return jnp.minimum(t, nv[0] - 1)

    def clamp(t, v, last, nv):
        return jnp.where(t < nv[0], v, last)

    grid_spec = pltpu.PrefetchScalarGridSpec(
        num_scalar_prefetch=3, grid=(MOE_TILES, gj, gk),
        in_specs=[
            pl.BlockSpec((MOE_BM, bk), lambda t, j, k, te, nr, nv: (row(t, nv), clamp(t, k, gk - 1, nv))),
            pl.BlockSpec((None, None, bk, bn),
                         lambda t, j, k, te, nr, nv: (moe_layer, te[t], clamp(t, k, gk - 1, nv),
                                                      clamp(t, j, gj - 1, nv))),
        ],
        out_specs=pl.BlockSpec((MOE_BM, bn), lambda t, j, k, te, nr, nv: (t, j)),
    )
    return pl.pallas_call(
        _moe_down_body, grid_spec=grid_spec,
        out_shape=jax.ShapeDtypeStruct((mid.shape[0], D_MODEL), F32),
        compiler_params=_params(3), name="moe_down",
    )(tile_e, tile_rows, n_valid, mid, w2)


def _combine_body(pos_ref, y_hbm, wt_ref, x_ref, g_ref, o_ref, buf, sem):
    base = pl.program_id(0) * GATHER_ROWS

    def row_copy(r, c, src_row):
        return pltpu.make_async_copy(y_hbm.at[pl.ds(src_row, 1)], buf.at[c, pl.ds(r, 1)], sem)

    def start(g, carry):
        for u in range(GATHER_UNROLL):
            r = g * GATHER_UNROLL + u
            for c in range(TOP_K):
                row_copy(r, c, pos_ref[(base + r) * TOP_K + c]).start()
        return carry

    def wait(g, carry):
        for u in range(GATHER_UNROLL):
            for c in range(TOP_K):
                row_copy(g * GATHER_UNROLL + u, c, 0).wait()
        return carry

    lax.fori_loop(0, GATHER_ROWS // GATHER_UNROLL, start, 0)
    lax.fori_loop(0, GATHER_ROWS // GATHER_UNROLL, wait, 0)
    wt = wt_ref[...]
    f = wt[:, 0:1] * buf[0] + wt[:, 1:2] * buf[1]
    o_ref[...] = x_ref[...] + g_ref[...] * f


def _moe_combine(y, pos, wts, x, mods, k_gate):
    m = x.shape[0]
    row = pl.BlockSpec((GATHER_ROWS, D_MODEL), lambda i, p: (i, 0))
    grid_spec = pltpu.PrefetchScalarGridSpec(
        num_scalar_prefetch=1, grid=(m // GATHER_ROWS,),
        in_specs=[
            pl.BlockSpec(memory_space=pl.ANY),
            pl.BlockSpec((GATHER_ROWS, LANES), lambda i, p: (i, 0)),
            row,
            pl.BlockSpec((None, 1, D_MODEL), lambda i, p: (_cond_of_tile(i, GATHER_ROWS) * 6 + k_gate, 0, 0)),
        ],
        out_specs=row,
        scratch_shapes=[pltpu.VMEM((TOP_K, GATHER_ROWS, D_MODEL), F32), pltpu.SemaphoreType.DMA(())],
    )
    return pl.pallas_call(
        _combine_body, grid_spec=grid_spec,
        out_shape=jax.ShapeDtypeStruct((m, D_MODEL), F32),
        compiler_params=_params(1), name="moe_combine",
    )(pos, y, wts, x, mods)


def _rope_tables():
    n_rows = DEC_SEQ // GRID_W
    row = jnp.repeat(jnp.arange(n_rows), GRID_W).astype(F32)
    col = jnp.tile(jnp.arange(GRID_W), n_rows).astype(F32)
    half = QK_ROPE // 2
    inv = ROPE_BASE ** (-(jnp.arange(half // 2, dtype=F32) * 2.0) / half)
    ang_r = row[:, None] * inv
    ang_c = col[:, None] * inv
    ang = jnp.concatenate([ang_r, ang_r, ang_c, ang_c], axis=-1)
    pad1 = jnp.ones((DEC_SEQ, LANES - QK_ROPE), F32)
    cos = jnp.concatenate([jnp.cos(ang), pad1], axis=1)
    sin = jnp.concatenate([jnp.sin(ang), 0 * pad1], axis=1)
    return cos, sin


def _mod_spec(bm, bn, which):
    return ((None, 1, bn), lambda i, j: (_cond_of_tile(i, bm) * 6 + which, 0, j))


def kernel(x_prompt, x_sample, cache_ckv, cache_kpe, c, c_ctx, norm1_g, norm2_g, w_ada, b_ada, w_in,
           w_fourier_out, conv_w, w_conv_out, q_norm_g, w_uq, kv_norm_g, w_ukv, w_mla_out, w_o,
           ffn_w1, ffn_w3, ffn_w2, router_w, router_b, moe_w1, moe_w3, moe_w2, final_g):
    bm = 1024
    x = jnp.concatenate([x_prompt.reshape(N_PROMPT, D_MODEL), x_sample.reshape(N_SAMPLE, D_MODEL)], axis=0)
    w_in_t = jnp.swapaxes(w_in, 1, 2)

    cond8 = jnp.concatenate([c_ctx[None, :], c, jnp.zeros((8 - 1 - DEC_BATCH, D_MODEL), F32)], axis=0)
    mods_all = _ada(cond8, w_ada, b_ada)[:, :1 + DEC_BATCH].reshape(DEPTH, (1 + DEC_BATCH) * 6, 1, D_MODEL)

    cos, sin = _rope_tables()
    one = jnp.ones((N_PROMPT, LANES), F32)
    cos_q = jnp.concatenate([one] + [cos] * DEC_BATCH, axis=0)
    sin_q = jnp.concatenate([0 * one] + [sin] * DEC_BATCH, axis=0)
    one_p = jnp.ones((PAST_LEN, LANES), F32)
    cos_k = jnp.concatenate([one_p, cos] * DEC_BATCH + [one], axis=0)
    sin_k = jnp.concatenate([0 * one_p, sin] * DEC_BATCH + [0 * one], axis=0)
    tab_chan = _dft_tables(FOURIER_GROUP, 1.0)
    tab_prompt = _dft_tables(SEQ, -1.0)
    tab_sample = _dft_tables(DEC_SEQ, -1.0)

    ckv_out, kpe_out = [], []
    for layer in range(DEPTH):
        mods = mods_all[layer]
        uq = w_uq[layer].reshape(Q_LORA, N_HEADS, QK_NOPE + QK_ROPE)
        uq = jnp.concatenate([uq, jnp.zeros((Q_LORA, N_HEADS, HEAD_PAD - QK_NOPE - QK_ROPE), F32)], axis=-1)
        uq = uq.reshape(Q_LORA, N_HEADS * HEAD_PAD).astype(BF16)

        h = _norm_mod(x, norm1_g, layer, mods, 1, 0)
        z = _mm("in_proj", h, [(w_in_t, (layer,), 0, True)], _ep_plain, bm=bm, bn=512, n_out=Z_MAIN, out_dtype=F32)

        ab = _fourier_chan(z, tab_chan)
        yf_pre = jnp.concatenate([
            _fourier_seq(ab, tab_prompt, BATCH, SEQ, 0, SEQ, D_FOURIER),
            _fourier_seq(ab, tab_sample, DEC_BATCH, DEC_SEQ, N_PROMPT // DEC_SEQ, 512, 512)], axis=0)
        y_f = _mm("fourier_out", yf_pre, [(w_fourier_out, (layer,), 0, False)], _ep_plain,
                  bm=bm, bn=512, n_out=D_MODEL, out_dtype=F32)

        yc_pre = _conv(z, conv_w, layer)
        y_c = _mm("conv_out", yc_pre, [(w_conv_out, (layer,), 0, False)], _ep_plain,
                  bm=bm, bn=512, n_out=D_MODEL, out_dtype=F32)

        q_col = (D_FOURIER + 3 * D_CONV) // Q_LORA
        q = _mm("q_proj", z, [(uq, (), 0, False)], _ep_rope_q, bm=bm, bn=512, n_out=N_HEADS * HEAD_PAD,
                out_dtype=BF16, k=Q_LORA, x_kblk=q_col, prologue=_pro_rmsnorm,
                pro_extras=[(q_norm_g.reshape(DEPTH, 1, Q_LORA), (None, 1, Q_LORA), lambda i, j: (layer, 0, 0))],
                extras=[(cos_q, (bm, LANES), lambda i, j: (i, 0)), (sin_q, (bm, LANES), lambda i, j: (i, 0))])
        kv_col = (D_FOURIER + 3 * D_CONV + Q_LORA) // KV_LORA
        c_kv = _rmsnorm(z, kv_norm_g.reshape(DEPTH, 1, KV_LORA), layer, KV_LORA, kv_col)
        k_pe = z[:, GATE_COL0 - QK_ROPE:GATE_COL0]
        ckv_out.append(c_kv[:N_PROMPT].reshape(BATCH, SEQ, KV_LORA))
        kpe_out.append(k_pe[:N_PROMPT].reshape(BATCH, SEQ, QK_ROPE))
        ckv_s = jnp.concatenate([cache_ckv[:, layer], c_kv[N_PROMPT:].reshape(DEC_BATCH, DEC_SEQ, KV_LORA)], axis=1)
        ckv_ext = jnp.concatenate([ckv_s.reshape(-1, KV_LORA), c_kv[:N_PROMPT]], axis=0).astype(BF16)
        kpe_s = jnp.concatenate([cache_kpe[:, layer], k_pe[N_PROMPT:].reshape(DEC_BATCH, DEC_SEQ, QK_ROPE)], axis=1)
        kpe_ext = jnp.concatenate([kpe_s.reshape(-1, QK_ROPE), k_pe[:N_PROMPT]], axis=0)
        kpe_ext = jnp.concatenate([kpe_ext, jnp.zeros((N_KV_ROWS, LANES - QK_ROPE), F32)], axis=1)
        kpe_ext = _rope_k(kpe_ext, cos_k, sin_k)
        kv = _mm("kv_proj", ckv_ext, [(w_ukv, (layer,), 0, False)], _ep_plain,
                 bm=512, bn=1024, n_out=N_HEADS * (QK_NOPE + V_HEAD), out_dtype=BF16)
        attn = jnp.concatenate([
            _attention(q, kv, kpe_ext, BATCH, SEQ, SEQ, 0, DEC_BATCH * KV_LEN, SEQ, N_HEADS),
            _attention(q, kv, kpe_ext, DEC_BATCH, DEC_SEQ, KV_LEN, N_PROMPT, 0, 512, 2)], axis=0)
        y_a = _mm("mla_out", attn, [(w_mla_out, (layer,), 0, False)], _ep_plain,
                  bm=bm, bn=512, n_out=D_MODEL, out_dtype=F32)

        tile = ((bm, 256), lambda i, j: (i, j))
        merged = _mm("gate_merge", h,
                     [(w_in_t, (layer,), GATE_COL0 + b * D_MODEL, True) for b in range(N_BRANCH)], _ep_merge,
                     bm=bm, bn=256, n_out=D_MODEL, out_dtype=BF16, x_buffers=1,
                     extras=[(y_f,) + tile, (y_c,) + tile, (y_a,) + tile])
        x = _mm("out_proj", merged, [(w_o, (layer,), 0, False)], _ep_residual, bm=bm, bn=512, n_out=D_MODEL,
                out_dtype=F32, extras=[(x, (bm, 512), lambda i, j: (i, j)), (mods,) + _mod_spec(bm, 512, 2)])

        j = layer // 2
        if layer % 2 == 0:
            h2 = _norm_mod(x, norm2_g, layer, mods, 4, 3)
            mid = _mm("ffn_up", h2, [(ffn_w1, (j,), 0, False), (ffn_w3, (j,), 0, False)], _ep_swiglu,
                      bm=bm, bn=256, n_out=D_FF, out_dtype=BF16)
            x = _mm("ffn_down", mid, [(ffn_w2[j].astype(BF16), (), 0, False)], _ep_residual, bm=512, bn=256,
                    n_out=D_MODEL, out_dtype=F32,
                    extras=[(x, (512, 256), lambda i, j: (i, j)), (mods,) + _mod_spec(512, 256, 5)])
        else:
            rw = jnp.concatenate([router_w[j], jnp.zeros((D_MODEL, LANES - N_EXPERTS), F32)], axis=1)
            rb = jnp.concatenate([router_b[j], jnp.zeros((LANES - N_EXPERTS,), F32)])[None, :]
            h2f, wts, idx = _norm_mod(x, norm2_g, layer, mods, 4, 3, router=_split_bf16(rw) + (rb,))
            pos, src, tile_e, tile_rows, n_valid = _route(idx[:, :TOP_K])
            xs = _gather_rows(h2f, src, tile_rows)
            mid = _moe_up(xs, moe_w1, moe_w3, j, tile_e, tile_rows, n_valid)
            y = _moe_down(mid, moe_w2, j, tile_e, tile_rows, n_valid)
            x = _moe_combine(y, pos, wts, x, mods, 5)

    fg = final_g.reshape(1, 1, D_MODEL)
    y_prompt = _rmsnorm(x, fg, 0, D_MODEL, 0, bm=256, row0=0, rows=N_PROMPT)
    y_sample = _rmsnorm(x, fg, 0, D_MODEL, 0, bm=256, row0=N_PROMPT, rows=N_SAMPLE)
    return (y_prompt.reshape(BATCH, SEQ, D_MODEL),
            y_sample.reshape(DEC_BATCH, DEC_SEQ, D_MODEL),
            jnp.stack(ckv_out, axis=1),
            jnp.stack(kpe_out, axis=1))
```

```python
import functools

import jax
import jax.numpy as jnp
import numpy as np
from jax import lax
from jax.experimental import pallas as pl
from jax.experimental.pallas import tpu as pltpu

F32 = jnp.float32
BF16 = jnp.bfloat16

D_MODEL = 4096
BATCH = 16
SEQ = 256
DEPTH = 2
DEC_BATCH = 2
DEC_SEQ = 2048
PAST_LEN = 256
GRID_W = 64
N_FOURIER_GROUPS = 4
FOURIER_GROUP = 256
D_FOURIER = N_FOURIER_GROUPS * FOURIER_GROUP
D_CONV = 1024
N_HEADS = 16
Q_LORA = 1024
KV_LORA = 512
QK_NOPE = 128
QK_ROPE = 64
V_HEAD = 128
ROPE_BASE = 10000.0
N_BRANCH = 3
D_FF = 11008
N_EXPERTS = 8
TOP_K = 2
D_FF_EXPERT = 14336
EPS = 1e-6

N_PROMPT = BATCH * SEQ
N_SAMPLE = DEC_BATCH * DEC_SEQ
N_TOK = N_PROMPT + N_SAMPLE
KV_LEN = PAST_LEN + DEC_SEQ
N_KV_ROWS = DEC_BATCH * KV_LEN + N_PROMPT
GATE_COL0 = D_FOURIER + 3 * D_CONV + Q_LORA + KV_LORA + QK_ROPE
Z_MAIN = 6144
HEAD_PAD = 256
LANES = 128

VMEM_LIMIT = 56 * 1024 * 1024
MOE_SUB = 128
MOE_BM = 10 * MOE_SUB
MOE_TILES = (TOP_K * N_TOK) // MOE_BM + N_EXPERTS
GATHER_ROWS = 256


def _params(n_axes, vmem=VMEM_LIMIT):
    return pltpu.CompilerParams(dimension_semantics=("arbitrary",) * n_axes, vmem_limit_bytes=vmem)


def _cond_of_tile(i, bm):
    r0 = i * bm
    return jnp.where(r0 < N_PROMPT, 0, 1 + (r0 - N_PROMPT) // DEC_SEQ)


def _split_bf16(x):
    hi = x.astype(BF16)
    lo = (x - hi.astype(F32)).astype(BF16)
    return hi, lo


def _dot(a, b):
    return jnp.dot(a, b, preferred_element_type=F32)


def _dot3(a_hi, a_lo, b_hi, b_lo):
    return _dot(a_hi, b_hi) + _dot(a_hi, b_lo) + _dot(a_lo, b_hi)


def _ada_body(c_ref, w_ref, b_ref, o_ref):
    c = c_ref[...]
    s = (c * jax.nn.sigmoid(c)).astype(BF16)
    o_ref[...] = _dot(s, w_ref[...].astype(BF16)) + b_ref[...]


def _ada(cond8, w_ada, b_ada, bn=512):
    n = w_ada.shape[-1]
    return pl.pallas_call(
        _ada_body,
        grid=(DEPTH, n // bn),
        in_specs=[
            pl.BlockSpec((8, D_MODEL), lambda l, j: (0, 0)),
            pl.BlockSpec((None, D_MODEL, bn), lambda l, j: (l, 0, j)),
            pl.BlockSpec((None, 1, bn), lambda l, j: (l, 0, j)),
        ],
        out_specs=pl.BlockSpec((None, 8, bn), lambda l, j: (l, 0, j)),
        out_shape=jax.ShapeDtypeStruct((DEPTH, 8, n), F32),
        compiler_params=_params(2),
        name="ada",
    )(cond8, w_ada, b_ada.reshape(DEPTH, 1, n))


def _rms(x):
    return x * lax.rsqrt(jnp.mean(x * x, axis=-1, keepdims=True) + EPS)


def _norm_mod_body(x_ref, g_ref, sc_ref, sh_ref, o_ref):
    y = _rms(x_ref[...]) * g_ref[...]
    o_ref[...] = (y * (1 + sc_ref[...]) + sh_ref[...]).astype(o_ref.dtype)


def _norm_mod_router_body(x_ref, g_ref, sc_ref, sh_ref, rw_hi_ref, rw_lo_ref, rb_ref,
                          of_ref, wt_ref, ix_ref):
    y = _rms(x_ref[...]) * g_ref[...]
    h = y * (1 + sc_ref[...]) + sh_ref[...]
    of_ref[...] = h
    h_hi, h_lo = _split_bf16(h)
    logits = _dot3(h_hi, h_lo, rw_hi_ref[...], rw_lo_ref[...]) + rb_ref[...]
    lane = lax.broadcasted_iota(jnp.int32, logits.shape, 1)
    lanef = lane.astype(F32)
    neg = jnp.float32(-jnp.inf)
    lg = jnp.where(lane < N_EXPERTS, logits, neg)
    m1 = jnp.max(lg, axis=-1, keepdims=True)
    i1 = jnp.min(jnp.where(lg == m1, lanef, float(LANES)), axis=-1, keepdims=True)
    lg2 = jnp.where(lanef == i1, neg, lg)
    m2 = jnp.max(lg2, axis=-1, keepdims=True)
    i2 = jnp.min(jnp.where(lg2 == m2, lanef, float(LANES)), axis=-1, keepdims=True)
    e = jnp.exp(m2 - m1)
    w1 = 1.0 / (1.0 + e)
    w2 = e / (1.0 + e)
    wt_ref[...] = jnp.where(lane == 0, w1, jnp.where(lane == 1, w2, 0.0))
    ix_ref[...] = jnp.where(lane == 0, i1, jnp.where(lane == 1, i2, 0.0)).astype(jnp.int32)


def _norm_mod(x, gain, layer, mods, k_scale, k_shift, router=None, bm=256):
    m = x.shape[0]
    in_specs = [
        pl.BlockSpec((bm, D_MODEL), lambda i: (i, 0)),
        pl.BlockSpec((None, 1, D_MODEL), lambda i: (layer, 0, 0)),
        pl.BlockSpec((None, 1, D_MODEL), lambda i: (_cond_of_tile(i, bm) * 6 + k_scale, 0, 0)),
        pl.BlockSpec((None, 1, D_MODEL), lambda i: (_cond_of_tile(i, bm) * 6 + k_shift, 0, 0)),
    ]
    args = [x, gain.reshape(DEPTH, 1, D_MODEL), mods, mods]
    row = pl.BlockSpec((bm, D_MODEL), lambda i: (i, 0))
    if router is None:
        return pl.pallas_call(
            _norm_mod_body, grid=(m // bm,), in_specs=in_specs, out_specs=row,
            out_shape=jax.ShapeDtypeStruct((m, D_MODEL), BF16),
            compiler_params=_params(1), name="norm_mod",
        )(*args)
    rw_hi, rw_lo, rb = router
    small = pl.BlockSpec((bm, LANES), lambda i: (i, 0))
    in_specs += [
        pl.BlockSpec((D_MODEL, LANES), lambda i: (0, 0)),
        pl.BlockSpec((D_MODEL, LANES), lambda i: (0, 0)),
        pl.BlockSpec((1, LANES), lambda i: (0, 0)),
    ]
    return pl.pallas_call(
        _norm_mod_router_body, grid=(m // bm,), in_specs=in_specs,
        out_specs=[row, small, small],
        out_shape=[jax.ShapeDtypeStruct((m, D_MODEL), F32),
                   jax.ShapeDtypeStruct((m, LANES), F32), jax.ShapeDtypeStruct((m, LANES), jnp.int32)],
        compiler_params=_params(1), name="norm_mod_router",
    )(*args, rw_hi, rw_lo, rb)


def _rmsnorm_body(x_ref, g_ref, o_ref):
    o_ref[...] = (_rms(x_ref[...]) * g_ref[...]).astype(o_ref.dtype)


def _rmsnorm(x, gain3, lead, width, col_blk, bm=512, row0=0, rows=None):
    m = rows or x.shape[0]
    return pl.pallas_call(
        _rmsnorm_body, grid=(m // bm,),
        in_specs=[pl.BlockSpec((bm, width), lambda i: (row0 // bm + i, col_blk)),
                  pl.BlockSpec((None, 1, width), lambda i: (lead, 0, 0))],
        out_specs=pl.BlockSpec((bm, width), lambda i: (i, 0)),
        out_shape=jax.ShapeDtypeStruct((m, width), F32),
        compiler_params=_params(1), name="rmsnorm",
    )(x, gain3)


def _dot_nt(a, b):
    return lax.dot_general(a, b, (((1,), (1,)), ((), ())), preferred_element_type=F32)


def _mm(name, x, ws, epilogue, *, bm, bn, n_out, out_dtype, k=None, x_kblk=0, extras=(),
        prologue=None, pro_extras=(), x_buffers=2):
    m = x.shape[0]
    k = k or x.shape[1]
    n_w, n_e, n_p = len(ws), len(extras), len(pro_extras)
    x_mode = {} if x_buffers == 2 else {"pipeline_mode": pl.Buffered(x_buffers)}
    in_specs = [pl.BlockSpec((bm, k), lambda i, j: (i, x_kblk), **x_mode)]
    args = [x]
    for arr, blk, imap in pro_extras:
        in_specs.append(pl.BlockSpec(blk, imap))
        args.append(arr)
    for arr, lead, col0, transposed in ws:
        squeeze = (None,) * len(lead)
        if transposed:
            in_specs.append(pl.BlockSpec((pl.Element(1),) * len(lead) + (pl.Element(bn), pl.Element(k)),
                                         lambda i, j, lead=lead, col0=col0:
                                         lead + (pl.multiple_of(col0 + j * bn, 8), 0)))
        else:
            assert col0 % bn == 0
            in_specs.append(pl.BlockSpec(squeeze + (k, bn),
                                         lambda i, j, lead=lead, col0=col0: lead + (0, j + col0 // bn)))
        args.append(arr)
    for arr, blk, imap in extras:
        in_specs.append(pl.BlockSpec(blk, imap))
        args.append(arr)

    def body(*refs):
        x_ref = refs[0]
        p_refs = refs[1:1 + n_p]
        w_refs = refs[1 + n_p:1 + n_p + n_w]
        e_refs = refs[1 + n_p + n_w:1 + n_p + n_w + n_e]
        o_ref = refs[1 + n_p + n_w + n_e]
        if prologue is None:
            xv = x_ref[...]
        else:
            xs_ref = refs[-1]

            @pl.when(pl.program_id(1) == 0)
            def _():
                xs_ref[...] = prologue(x_ref, *p_refs)

            xv = xs_ref[...]
        accs = []
        for (_, lead, _, transposed), w_ref in zip(ws, w_refs):
            if transposed:
                accs.append(_dot_nt(xv, w_ref[(0,) * len(lead)].astype(BF16)))
            else:
                accs.append(_dot(xv, w_ref[...].astype(BF16)))
        o_ref[...] = epilogue(accs, *e_refs).astype(o_ref.dtype)

    scratch = [] if prologue is None else [pltpu.VMEM((bm, k), BF16)]
    return pl.pallas_call(
        body, grid=(m // bm, pl.cdiv(n_out, bn)), in_specs=in_specs,
        out_specs=pl.BlockSpec((bm, bn), lambda i, j: (i, j)),
        out_shape=jax.ShapeDtypeStruct((m, n_out), out_dtype),
        scratch_shapes=scratch, compiler_params=_params(2), name=name,
    )(*args)


def _ep_plain(accs):
    return accs[0]


def _ep_swiglu(accs):
    a, b = accs
    return (a * jax.nn.sigmoid(a)) * b


def _ep_residual(accs, x_ref, g_ref):
    return x_ref[...] + g_ref[...] * accs[0]


def _ep_merge(accs, yf_ref, yc_ref, ya_ref):
    g0, g1, g2 = (jax.nn.sigmoid(a) for a in accs)
    return g0 * yf_ref[...] + g1 * yc_ref[...] + g2 * ya_ref[...]


def _rot_half_pairs(x):
    lane = lax.broadcasted_iota(jnp.int32, x.shape, 1)
    first = (lane // 16) % 2 == 0
    return jnp.where(first, -pltpu.roll(x, LANES - 16, 1), pltpu.roll(x, 16, 1))


def _ep_rope_q(accs, cos_ref, sin_ref):
    acc = accs[0]
    cos, sin = cos_ref[...], sin_ref[...]
    parts = []
    for h in range(acc.shape[1] // HEAD_PAD):
        base = h * HEAD_PAD
        parts.append(acc[:, base:base + QK_NOPE])
        pe = acc[:, base + QK_NOPE:base + HEAD_PAD]
        parts.append(pe * cos + _rot_half_pairs(pe) * sin)
    return jnp.concatenate(parts, axis=1)


def _pro_rmsnorm(x_ref, g_ref):
    return (_rms(x_ref[...]) * g_ref[...]).astype(BF16)


def _dft_tables(n, sign):
    p = jnp.arange(n, dtype=jnp.int32)
    ang = ((p[:, None] * p[None, :]) % n).astype(F32) * np.float32(2.0 * np.pi / n)
    scale = np.float32(1.0 / np.sqrt(n))
    return _split_bf16(jnp.cos(ang) * scale) + _split_bf16(jnp.sin(ang) * (sign * scale))


def _fourier_chan_body(u_ref, c_hi, c_lo, s_hi, s_lo, a_hi, a_lo, b_hi, b_lo):
    for g in range(N_FOURIER_GROUPS):
        sl = slice(g * FOURIER_GROUP, (g + 1) * FOURIER_GROUP)
        u_h, u_l = _split_bf16(u_ref[:, sl])
        ah, al = _split_bf16(_dot3(u_h, u_l, c_hi[...], c_lo[...]))
        bh, bl = _split_bf16(_dot3(u_h, u_l, s_hi[...], s_lo[...]))
        a_hi[:, sl] = ah
        a_lo[:, sl] = al
        b_hi[:, sl] = bh
        b_lo[:, sl] = bl


def _fourier_chan(z, tabs, bm=512):
    m = z.shape[0]
    tab = pl.BlockSpec((FOURIER_GROUP, FOURIER_GROUP), lambda i: (0, 0))
    row = pl.BlockSpec((bm, D_FOURIER), lambda i: (i, 0))
    return pl.pallas_call(
        _fourier_chan_body, grid=(m // bm,),
        in_specs=[row, tab, tab, tab, tab], out_specs=[row] * 4,
        out_shape=[jax.ShapeDtypeStruct((m, D_FOURIER), BF16)] * 4,
        compiler_params=_params(1), name="fourier_chan",
    )(z, *tabs)


def _fourier_seq_body(c_hi, c_lo, s_hi, s_lo, a_hi, a_lo, b_hi, b_lo, o_ref):
    y = _dot3(c_hi[...], c_lo[...], a_hi[...], a_lo[...]) + _dot3(s_hi[...], s_lo[...], b_hi[...], b_lo[...])
    o_ref[...] = y.astype(o_ref.dtype)


def _fourier_seq(ab, tabs, n_seq, seq_len, seq_blk0, bl, bn):
    nr, nc = seq_len // bl, D_FOURIER // bn
    tab = pl.BlockSpec((bl, seq_len), lambda s, c, r: (r, 0))
    dat = pl.BlockSpec((seq_len, bn), lambda s, c, r: (seq_blk0 + s, c))
    return pl.pallas_call(
        _fourier_seq_body, grid=(n_seq, nc, nr),
        in_specs=[tab] * 4 + [dat] * 4,
        out_specs=pl.BlockSpec((bl, bn), lambda s, c, r: (s * nr + r, c)),
        out_shape=jax.ShapeDtypeStruct((n_seq * seq_len, D_FOURIER), BF16),
        compiler_params=_params(3), name=f"fourier_seq{seq_len}",
    )(*tabs, *ab)


CONV_ROWS = 2048


def _conv_body(b_ref, c_ref, h_ref, w_ref, o_ref):
    i = pl.program_id(0)
    seq_len = jnp.where(i * CONV_ROWS < N_PROMPT, SEQ, DEC_SEQ)
    u = c_ref[...] * h_ref[...]
    pos = lax.broadcasted_iota(jnp.int32, u.shape, 0) & (seq_len - 1)
    prev = jnp.where(pos == 0, 0.0, pltpu.roll(u, 1, 0))
    nxt = jnp.where(pos == seq_len - 1, 0.0, pltpu.roll(u, CONV_ROWS - 1, 0))
    w = w_ref[...]
    o_ref[...] = (b_ref[...] * (w[0:1] * prev + w[1:2] * u + w[2:3] * nxt)).astype(o_ref.dtype)


def _conv(z, conv_w, layer, bc=256):
    m = z.shape[0]
    nb = D_CONV // bc
    col0 = D_FOURIER // bc

    def part(k):
        return pl.BlockSpec((CONV_ROWS, bc), lambda i, j: (i, col0 + k * nb + j))

    return pl.pallas_call(
        _conv_body, grid=(m // CONV_ROWS, nb),
        in_specs=[part(0), part(1), part(2), pl.BlockSpec((None, 3, bc), lambda i, j: (layer, 0, j))],
        out_specs=pl.BlockSpec((CONV_ROWS, bc), lambda i, j: (i, j)),
        out_shape=jax.ShapeDtypeStruct((m, D_CONV), BF16),
        compiler_params=_params(2), name="short_conv",
    )(z, z, z, conv_w)


def _rope_k_body(x_ref, cos_ref, sin_ref, o_ref):
    x = x_ref[...]
    o_ref[...] = (x * cos_ref[...] + _rot_half_pairs(x) * sin_ref[...]).astype(o_ref.dtype)


def _rope_k(x, cos, sin, bm=512):
    m = x.shape[0]
    blk = pl.BlockSpec((bm, LANES), lambda i: (i, 0))
    return pl.pallas_call(
        _rope_k_body, grid=(m // bm,), in_specs=[blk, blk, blk], out_specs=blk,
        out_shape=jax.ShapeDtypeStruct((m, LANES), BF16),
        compiler_params=_params(1), name="rope_k",
    )(x, cos, sin)


def _attn_body(q_ref, kv_ref, kp_ref, o_ref, kcat_ref, *, heads):
    @pl.when(pl.program_id(2) == 0)
    def _():
        kp = kp_ref[...]
        for h in range(heads):
            kcat_ref[h, :, :QK_NOPE] = kv_ref[:, h * HEAD_PAD:h * HEAD_PAD + QK_NOPE]
            kcat_ref[h, :, QK_NOPE:] = kp

    for h in range(heads):
        q = q_ref[:, h * HEAD_PAD:(h + 1) * HEAD_PAD]
        v = kv_ref[:, h * HEAD_PAD + QK_NOPE:(h + 1) * HEAD_PAD]
        s = _dot_nt(q, kcat_ref[h]) * np.float32((QK_NOPE + QK_ROPE) ** -0.5)
        p = jnp.exp(s - jnp.max(s, axis=-1, keepdims=True))
        l = jnp.sum(p, axis=-1, keepdims=True)
        o_ref[:, h * V_HEAD:(h + 1) * V_HEAD] = (_dot(p.astype(BF16), v) / l).astype(o_ref.dtype)


def _attention(q, kv, kpe, n_seq, lq, lk, q_row0, k_row0, bq, heads):
    nq = lq // bq
    return pl.pallas_call(
        functools.partial(_attn_body, heads=heads), grid=(n_seq, N_HEADS // heads, nq),
        in_specs=[
            pl.BlockSpec((bq, heads * HEAD_PAD), lambda b, h, i: (q_row0 // bq + b * nq + i, h)),
            pl.BlockSpec((lk, heads * (QK_NOPE + V_HEAD)), lambda b, h, i: (k_row0 // lk + b, h)),
            pl.BlockSpec((lk, LANES), lambda b, h, i: (k_row0 // lk + b, 0)),
        ],
        out_specs=pl.BlockSpec((bq, heads * V_HEAD), lambda b, h, i: (b * nq + i, h)),
        out_shape=jax.ShapeDtypeStruct((n_seq * lq, N_HEADS * V_HEAD), BF16),
        scratch_shapes=[pltpu.VMEM((heads, lk, HEAD_PAD), BF16)],
        compiler_params=_params(3), name=f"attention{lq}",
    )(q, kv, kpe)


def _route(idx):
    flat_e = idx.reshape(-1)
    n = flat_e.shape[0]
    onehot = (flat_e[:, None] == jnp.arange(N_EXPERTS, dtype=jnp.int32)[None, :]).astype(jnp.int32)
    csum = jnp.cumsum(onehot, axis=0)
    rank = jnp.take_along_axis(csum, flat_e[:, None], axis=1)[:, 0] - 1
    counts = csum[-1]
    n_tile = (counts + MOE_BM - 1) // MOE_BM
    tile_end = jnp.cumsum(n_tile)
    tile_start = tile_end - n_tile
    pos = tile_start[flat_e] * MOE_BM + rank
    n_valid = tile_end[-1]
    t = jnp.arange(MOE_TILES, dtype=jnp.int32)
    tc = jnp.minimum(t, n_valid - 1)
    tile_e = jnp.sum((tile_end[None, :] <= tc[:, None]).astype(jnp.int32), axis=1)
    rows = jnp.clip(counts[tile_e] - (tc - tile_start[tile_e]) * MOE_BM, 0, MOE_BM)
    tile_rows = jnp.where(t < n_valid, rows, 0)
    src = jnp.zeros((MOE_TILES * MOE_BM,), jnp.int32).at[pos].set(jnp.arange(n, dtype=jnp.int32) // TOP_K)
    i32 = jnp.int32
    return pos.astype(i32), src, tile_e.astype(i32), tile_rows.astype(i32), n_valid.reshape(1).astype(i32)


GATHER_UNROLL = 8


def _gather_body(src_ref, nr_ref, h_hbm, o_ref, buf, sem):
    t, s = pl.program_id(0), pl.program_id(1)
    base = t * MOE_BM + s * GATHER_ROWS
    active = s * GATHER_ROWS < nr_ref[t]

    def row_copy(r, src_row):
        return pltpu.make_async_copy(h_hbm.at[pl.ds(src_row, 1)], buf.at[pl.ds(r, 1)], sem)

    @pl.when(active)
    def _():
        def start(g, c):
            for u in range(GATHER_UNROLL):
                r = g * GATHER_UNROLL + u
                row_copy(r, src_ref[base + r]).start()
            return c

        def wait(g, c):
            for u in range(GATHER_UNROLL):
                row_copy(g * GATHER_UNROLL + u, 0).wait()
            return c

        lax.fori_loop(0, GATHER_ROWS // GATHER_UNROLL, start, 0)
        lax.fori_loop(0, GATHER_ROWS // GATHER_UNROLL, wait, 0)
        o_ref[...] = buf[...].astype(o_ref.dtype)

    @pl.when(jnp.logical_not(active))
    def _():
        o_ref[...] = jnp.zeros_like(o_ref)


def _gather_rows(h, src, tile_rows):
    n_sub = MOE_BM // GATHER_ROWS
    grid_spec = pltpu.PrefetchScalarGridSpec(
        num_scalar_prefetch=2, grid=(MOE_TILES, n_sub),
        in_specs=[pl.BlockSpec(memory_space=pl.ANY)],
        out_specs=pl.BlockSpec((GATHER_ROWS, D_MODEL), lambda t, s, src, nr: (t * n_sub + s, 0)),
        scratch_shapes=[pltpu.VMEM((GATHER_ROWS, D_MODEL), F32), pltpu.SemaphoreType.DMA(())],
    )
    return pl.pallas_call(
        _gather_body, grid_spec=grid_spec,
        out_shape=jax.ShapeDtypeStruct((MOE_TILES * MOE_BM, D_MODEL), BF16),
        compiler_params=_params(2), name="moe_gather",
    )(src, tile_rows, h)


def _sub_blocks(nr_ref):
    return (nr_ref[pl.program_id(0)] + MOE_SUB - 1) // MOE_SUB


def _moe_up_body(te_ref, nr_ref, nv_ref, x_ref, w1_ref, w3_ref, o_ref):
    n_sub = _sub_blocks(nr_ref)
    for n in range(MOE_BM // MOE_SUB + 1):
        @pl.when(n_sub == n)
        def _(n=n):
            m = n * MOE_SUB
            if m:
                xv = x_ref[:m, :]
                a = _dot(xv, w1_ref[...].astype(BF16))
                b = _dot(xv, w3_ref[...].astype(BF16))
                o_ref[:m, :] = ((a * jax.nn.sigmoid(a)) * b).astype(o_ref.dtype)
            if m < MOE_BM:
                o_ref[m:, :] = jnp.zeros((MOE_BM - m, o_ref.shape[1]), o_ref.dtype)


def _moe_up(xs, w1, w3, moe_layer, tile_e, tile_rows, n_valid, bn=256):
    gj = D_FF_EXPERT // bn

    def row(t, nv):
        return jnp.minimum(t, nv[0] - 1)

    def col(t, j, nv):
        return jnp.where(t < nv[0], j, gj - 1)

    w_spec = pl.BlockSpec((None, None, D_MODEL, bn),
                          lambda t, j, te, nr, nv: (moe_layer, te[t], 0, col(t, j, nv)))
    grid_spec = pltpu.PrefetchScalarGridSpec(
        num_scalar_prefetch=3, grid=(MOE_TILES, gj),
        in_specs=[pl.BlockSpec((MOE_BM, D_MODEL), lambda t, j, te, nr, nv: (row(t, nv), 0)), w_spec, w_spec],
        out_specs=pl.BlockSpec((MOE_BM, bn), lambda t, j, te, nr, nv: (t, j)),
    )
    return pl.pallas_call(
        _moe_up_body, grid_spec=grid_spec,
        out_shape=jax.ShapeDtypeStruct((xs.shape[0], D_FF_EXPERT), BF16),
        compiler_params=_params(2), name="moe_up",
    )(tile_e, tile_rows, n_valid, xs, w1, w3)


def _moe_down_body(te_ref, nr_ref, nv_ref, x_ref, w_ref, o_ref):
    n_sub = _sub_blocks(nr_ref)
    first = pl.program_id(2) == 0
    for n in range(MOE_BM // MOE_SUB + 1):
        m = n * MOE_SUB

        @pl.when((n_sub == n) & first)
        def _(m=m):
            if m:
                o_ref[:m, :] = _dot(x_ref[:m, :], w_ref[...].astype(BF16))
            if m < MOE_BM:
                o_ref[m:, :] = jnp.zeros((MOE_BM - m, o_ref.shape[1]), o_ref.dtype)

        if m:
            @pl.when((n_sub == n) & jnp.logical_not(first))
            def _(m=m):
                o_ref[:m, :] += _dot(x_ref[:m, :], w_ref[...].astype(BF16))


def _moe_down(mid, w2, moe_layer, tile_e, tile_rows, n_valid, bn=1024, bk=2048):
    gj, gk = D_MODEL // bn, D_FF_EXPERT // bk

    def row(t, nv):
        return jnp.minimum(t, nv[0] - 1)

    def clamp(t, v, last, nv):
        return jnp.where(t < nv[0], v, last)

    grid_spec = pltpu.PrefetchScalarGridSpec(
        num_scalar_prefetch=3, grid=(MOE_TILES, gj, gk),
        in_specs=[
            pl.BlockSpec((MOE_BM, bk), lambda t, j, k, te, nr, nv: (row(t, nv), clamp(t, k, gk - 1, nv))),
            pl.BlockSpec((None, None, bk, bn),
                         lambda t, j, k, te, nr, nv: (moe_layer, te[t], clamp(t, k, gk - 1, nv),
                                                      clamp(t, j, gj - 1, nv))),
        ],
        out_specs=pl.BlockSpec((MOE_BM, bn), lambda t, j, k, te, nr, nv: (t, j)),
    )
    return pl.pallas_call(
        _moe_down_body, grid_spec=grid_spec,
        out_shape=jax.ShapeDtypeStruct((mid.shape[0], D_MODEL), F32),
        compiler_params=_params(3), name="moe_down",
    )(tile_e, tile_rows, n_valid, mid, w2)


def _combine_body(pos_ref, y_hbm, wt_ref, x_ref, g_ref, o_ref, buf, sem):
    base = pl.program_id(0) * GATHER_ROWS

    def row_copy(r, c, src_row):
        return pltpu.make_async_copy(y_hbm.at[pl.ds(src_row, 1)], buf.at[c, pl.ds(r, 1)], sem)

    def start(g, carry):
        for u in range(GATHER_UNROLL):
            r = g * GATHER_UNROLL + u
            for c in range(TOP_K):
                row_copy(r, c, pos_ref[(base + r) * TOP_K + c]).start()
        return carry

    def wait(g, carry):
        for u in range(GATHER_UNROLL):
            for c in range(TOP_K):
                row_copy(g * GATHER_UNROLL + u, c, 0).wait()
        return carry

    lax.fori_loop(0, GATHER_ROWS // GATHER_UNROLL, start, 0)
    lax.fori_loop(0, GATHER_ROWS // GATHER_UNROLL, wait, 0)
    wt = wt_ref[...]
    f = wt[:, 0:1] * buf[0] + wt[:, 1:2] * buf[1]
    o_ref[...] = x_ref[...] + g_ref[...] * f


def _moe_combine(y, pos, wts, x, mods, k_gate):
    m = x.shape[0]
    row = pl.BlockSpec((GATHER_ROWS, D_MODEL), lambda i, p: (i, 0))
    grid_spec = pltpu.PrefetchScalarGridSpec(
        num_scalar_prefetch=1, grid=(m // GATHER_ROWS,),
        in_specs=[
            pl.BlockSpec(memory_space=pl.ANY),
            pl.BlockSpec((GATHER_ROWS, LANES), lambda i, p: (i, 0)),
            row,
            pl.BlockSpec((None, 1, D_MODEL), lambda i, p: (_cond_of_tile(i, GATHER_ROWS) * 6 + k_gate, 0, 0)),
        ],
        out_specs=row,
        scratch_shapes=[pltpu.VMEM((TOP_K, GATHER_ROWS, D_MODEL), F32), pltpu.SemaphoreType.DMA(())],
    )
    return pl.pallas_call(
        _combine_body, grid_spec=grid_spec,
        out_shape=jax.ShapeDtypeStruct((m, D_MODEL), F32),
        compiler_params=_params(1), name="moe_combine",
    )(pos, y, wts, x, mods)


def _rope_tables():
    n_rows = DEC_SEQ // GRID_W
    row = jnp.repeat(jnp.arange(n_rows), GRID_W).astype(F32)
    col = jnp.tile(jnp.arange(GRID_W), n_rows).astype(F32)
    half = QK_ROPE // 2
    inv = ROPE_BASE ** (-(jnp.arange(half // 2, dtype=F32) * 2.0) / half)
    ang_r = row[:, None] * inv
    ang_c = col[:, None] * inv
    ang = jnp.concatenate([ang_r, ang_r, ang_c, ang_c], axis=-1)
    pad1 = jnp.ones((DEC_SEQ, LANES - QK_ROPE), F32)
    cos = jnp.concatenate([jnp.cos(ang), pad1], axis=1)
    sin = jnp.concatenate([jnp.sin(ang), 0 * pad1], axis=1)
    return cos, sin


def _mod_spec(bm, bn, which):
    return ((None, 1, bn), lambda i, j: (_cond_of_tile(i, bm) * 6 + which, 0, j))


def kernel(x_prompt, x_sample, cache_ckv, cache_kpe, c, c_ctx, norm1_g, norm2_g, w_ada, b_ada, w_in,
           w_fourier_out, conv_w, w_conv_out, q_norm_g, w_uq, kv_norm_g, w_ukv, w_mla_out, w_o,
           ffn_w1, ffn_w3, ffn_w2, router_w, router_b, moe_w1, moe_w3, moe_w2, final_g):
    bm = 1024
    x = jnp.concatenate([x_prompt.reshape(N_PROMPT, D_MODEL), x_sample.reshape(N_SAMPLE, D_MODEL)], axis=0)
    w_in_t = jnp.swapaxes(w_in, 1, 2)

    cond8 = jnp.concatenate([c_ctx[None, :], c, jnp.zeros((8 - 1 - DEC_BATCH, D_MODEL), F32)], axis=0)
    mods_all = _ada(cond8, w_ada, b_ada)[:, :1 + DEC_BATCH].reshape(DEPTH, (1 + DEC_BATCH) * 6, 1, D_MODEL)

    cos, sin = _rope_tables()
    one = jnp.ones((N_PROMPT, LANES), F32)
    cos_q = jnp.concatenate([one] + [cos] * DEC_BATCH, axis=0)
    sin_q = jnp.concatenate([0 * one] + [sin] * DEC_BATCH, axis=0)
    one_p = jnp.ones((PAST_LEN, LANES), F32)
    cos_k = jnp.concatenate([one_p, cos] * DEC_BATCH + [one], axis=0)
    sin_k = jnp.concatenate([0 * one_p, sin] * DEC_BATCH + [0 * one], axis=0)
    tab_chan = _dft_tables(FOURIER_GROUP, 1.0)
    tab_prompt = _dft_tables(SEQ, -1.0)
    tab_sample = _dft_tables(DEC_SEQ, -1.0)

    ckv_out, kpe_out = [], []
    for layer in range(DEPTH):
        mods = mods_all[layer]
        uq = w_uq[layer].reshape(Q_LORA, N_HEADS, QK_NOPE + QK_ROPE)
        uq = jnp.concatenate([uq, jnp.zeros((Q_LORA, N_HEADS, HEAD_PAD - QK_NOPE - QK_ROPE), F32)], axis=-1)
        uq = uq.reshape(Q_LORA, N_HEADS * HEAD_PAD).astype(BF16)

        h = _norm_mod(x, norm1_g, layer, mods, 1, 0)
        z = _mm("in_proj", h, [(w_in_t, (layer,), 0, True)], _ep_plain, bm=bm, bn=512, n_out=Z_MAIN, out_dtype=F32)

        ab = _fourier_chan(z, tab_chan)
        yf_pre = jnp.concatenate([
            _fourier_seq(ab, tab_prompt, BATCH, SEQ, 0, SEQ, D_FOURIER),
            _fourier_seq(ab, tab_sample, DEC_BATCH, DEC_SEQ, N_PROMPT // DEC_SEQ, 512, 512)], axis=0)
        y_f = _mm("fourier_out", yf_pre, [(w_fourier_out, (layer,), 0, False)], _ep_plain,
                  bm=bm, bn=2048, n_out=D_MODEL, out_dtype=F32)

        yc_pre = _conv(z, conv_w, layer)
        y_c = _mm("conv_out", yc_pre, [(w_conv_out, (layer,), 0, False)], _ep_plain,
                  bm=bm, bn=2048, n_out=D_MODEL, out_dtype=F32)

        q_col = (D_FOURIER + 3 * D_CONV) // Q_LORA
        q = _mm("q_proj", z, [(uq, (), 0, False)], _ep_rope_q, bm=bm, bn=512, n_out=N_HEADS * HEAD_PAD,
                out_dtype=BF16, k=Q_LORA, x_kblk=q_col, prologue=_pro_rmsnorm,
                pro_extras=[(q_norm_g.reshape(DEPTH, 1, Q_LORA), (None, 1, Q_LORA), lambda i, j: (layer, 0, 0))],
                extras=[(cos_q, (bm, LANES), lambda i, j: (i, 0)), (sin_q, (bm, LANES), lambda i, j: (i, 0))])
        kv_col = (D_FOURIER + 3 * D_CONV + Q_LORA) // KV_LORA
        c_kv = _rmsnorm(z, kv_norm_g.reshape(DEPTH, 1, KV_LORA), layer, KV_LORA, kv_col)
        k_pe = z[:, GATE_COL0 - QK_ROPE:GATE_COL0]
        ckv_out.append(c_kv[:N_PROMPT].reshape(BATCH, SEQ, KV_LORA))
        kpe_out.append(k_pe[:N_PROMPT].reshape(BATCH, SEQ, QK_ROPE))
        ckv_s = jnp.concatenate([cache_ckv[:, layer], c_kv[N_PROMPT:].reshape(DEC_BATCH, DEC_SEQ, KV_LORA)], axis=1)
        ckv_ext = jnp.concatenate([ckv_s.reshape(-1, KV_LORA), c_kv[:N_PROMPT]], axis=0).astype(BF16)
        kpe_s = jnp.concatenate([cache_kpe[:, layer], k_pe[N_PROMPT:].reshape(DEC_BATCH, DEC_SEQ, QK_ROPE)], axis=1)
        kpe_ext = jnp.concatenate([kpe_s.reshape(-1, QK_ROPE), k_pe[:N_PROMPT]], axis=0)
        kpe_ext = jnp.concatenate([kpe_ext, jnp.zeros((N_KV_ROWS, LANES - QK_ROPE), F32)], axis=1)
        kpe_ext = _rope_k(kpe_ext, cos_k, sin_k)
        kv = _mm("kv_proj", ckv_ext, [(w_ukv, (layer,), 0, False)], _ep_plain,
                 bm=512, bn=2048, n_out=N_HEADS * (QK_NOPE + V_HEAD), out_dtype=BF16)
        attn = jnp.concatenate([
            _attention(q, kv, kpe_ext, BATCH, SEQ, SEQ, 0, DEC_BATCH * KV_LEN, SEQ, N_HEADS),
            _attention(q, kv, kpe_ext, DEC_BATCH, DEC_SEQ, KV_LEN, N_PROMPT, 0, 512, 4)], axis=0)
        y_a = _mm("mla_out", attn, [(w_mla_out, (layer,), 0, False)], _ep_plain,
                  bm=bm, bn=1024, n_out=D_MODEL, out_dtype=F32)

        tile = ((bm, 256), lambda i, j: (i, j))
        merged = _mm("gate_merge", h,
                     [(w_in_t, (layer,), GATE_COL0 + b * D_MODEL, True) for b in range(N_BRANCH)], _ep_merge,
                     bm=bm, bn=256, n_out=D_MODEL, out_dtype=BF16, x_buffers=1,
                     extras=[(y_f,) + tile, (y_c,) + tile, (y_a,) + tile])
        x = _mm("out_proj", merged, [(w_o, (layer,), 0, False)], _ep_residual, bm=bm, bn=512, n_out=D_MODEL,
                out_dtype=F32, extras=[(x, (bm, 512), lambda i, j: (i, j)), (mods,) + _mod_spec(bm, 512, 2)])

        j = layer // 2
        if layer % 2 == 0:
            h2 = _norm_mod(x, norm2_g, layer, mods, 4, 3)
            mid = _mm("ffn_up", h2, [(ffn_w1, (j,), 0, False), (ffn_w3, (j,), 0, False)], _ep_swiglu,
                      bm=bm, bn=256, n_out=D_FF, out_dtype=BF16)
            x = _mm("ffn_down", mid, [(ffn_w2[j].astype(BF16), (), 0, False)], _ep_residual, bm=512, bn=256,
                    n_out=D_MODEL, out_dtype=F32,
                    extras=[(x, (512, 256), lambda i, j: (i, j)), (mods,) + _mod_spec(512, 256, 5)])
        else:
            rw = jnp.concatenate([router_w[j], jnp.zeros((D_MODEL, LANES - N_EXPERTS), F32)], axis=1)
            rb = jnp.concatenate([router_b[j], jnp.zeros((LANES - N_EXPERTS,), F32)])[None, :]
            h2f, wts, idx = _norm_mod(x, norm2_g, layer, mods, 4, 3, router=_split_bf16(rw) + (rb,))
            pos, src, tile_e, tile_rows, n_valid = _route(idx[:, :TOP_K])
            xs = _gather_rows(h2f, src, tile_rows)
            mid = _moe_up(xs, moe_w1, moe_w3, j, tile_e, tile_rows, n_valid)
            y = _moe_down(mid, moe_w2, j, tile_e, tile_rows, n_valid)
            x = _moe_combine(y, pos, wts, x, mods, 5)

    fg = final_g.reshape(1, 1, D_MODEL)
    y_prompt = _rmsnorm(x, fg, 0, D_MODEL, 0, bm=256, row0=0, rows=N_PROMPT)
    y_sample = _rmsnorm(x, fg, 0, D_MODEL, 0, bm=256, row0=N_PROMPT, rows=N_SAMPLE)
    return (y_prompt.reshape(BATCH, SEQ, D_MODEL),
            y_sample.reshape(DEC_BATCH, DEC_SEQ, D_MODEL),
            jnp.stack(ckv_out, axis=1),
            jnp.stack(kpe_out, axis=1))
```

```python
import functools

import jax
import jax.numpy as jnp
import numpy as np
from jax import lax
from jax.experimental import pallas as pl
from jax.experimental.pallas import tpu as pltpu

F32 = jnp.float32
BF16 = jnp.bfloat16

D_MODEL = 4096
BATCH = 16
SEQ = 256
DEPTH = 2
DEC_BATCH = 2
DEC_SEQ = 2048
PAST_LEN = 256
GRID_W = 64
N_FOURIER_GROUPS = 4
FOURIER_GROUP = 256
D_FOURIER = N_FOURIER_GROUPS * FOURIER_GROUP
D_CONV = 1024
N_HEADS = 16
Q_LORA = 1024
KV_LORA = 512
QK_NOPE = 128
QK_ROPE = 64
V_HEAD = 128
ROPE_BASE = 10000.0
N_BRANCH = 3
D_FF = 11008
N_EXPERTS = 8
TOP_K = 2
D_FF_EXPERT = 14336
EPS = 1e-6

N_PROMPT = BATCH * SEQ
N_SAMPLE = DEC_BATCH * DEC_SEQ
N_TOK = N_PROMPT + N_SAMPLE
KV_LEN = PAST_LEN + DEC_SEQ
N_KV_ROWS = DEC_BATCH * KV_LEN + N_PROMPT
GATE_COL0 = D_FOURIER + 3 * D_CONV + Q_LORA + KV_LORA + QK_ROPE
Z_MAIN = 6144
HEAD_PAD = 256
LANES = 128

VMEM_LIMIT = 56 * 1024 * 1024
MOE_SUB = 256
MOE_BM = 5 * MOE_SUB
MOE_TILES = (TOP_K * N_TOK) // MOE_BM + N_EXPERTS
GATHER_ROWS = 256


def _params(n_axes, vmem=VMEM_LIMIT):
    return pltpu.CompilerParams(dimension_semantics=("arbitrary",) * n_axes, vmem_limit_bytes=vmem)


def _cond_of_tile(i, bm):
    r0 = i * bm
    return jnp.where(r0 < N_PROMPT, 0, 1 + (r0 - N_PROMPT) // DEC_SEQ)


def _split_bf16(x):
    hi = x.astype(BF16)
    lo = (x - hi.astype(F32)).astype(BF16)
    return hi, lo


def _dot(a, b):
    return jnp.dot(a, b, preferred_element_type=F32)


def _dot3(a_hi, a_lo, b_hi, b_lo):
    return _dot(a_hi, b_hi) + _dot(a_hi, b_lo) + _dot(a_lo, b_hi)


def _ada_body(c_ref, w_ref, b_ref, o_ref):
    c = c_ref[...]
    s = (c * jax.nn.sigmoid(c)).astype(BF16)
    o_ref[...] = _dot(s, w_ref[...].astype(BF16)) + b_ref[...]


def _ada(cond8, w_ada, b_ada, bn=512):
    n = w_ada.shape[-1]
    return pl.pallas_call(
        _ada_body,
        grid=(DEPTH, n // bn),
        in_specs=[
            pl.BlockSpec((8, D_MODEL), lambda l, j: (0, 0)),
            pl.BlockSpec((None, D_MODEL, bn), lambda l, j: (l, 0, j)),
            pl.BlockSpec((None, 1, bn), lambda l, j: (l, 0, j)),
        ],
        out_specs=pl.BlockSpec((None, 8, bn), lambda l, j: (l, 0, j)),
        out_shape=jax.ShapeDtypeStruct((DEPTH, 8, n), F32),
        compiler_params=_params(2),
        name="ada",
    )(cond8, w_ada, b_ada.reshape(DEPTH, 1, n))


def _rms(x):
    return x * lax.rsqrt(jnp.mean(x * x, axis=-1, keepdims=True) + EPS)


def _norm_mod_body(x_ref, g_ref, sc_ref, sh_ref, o_ref):
    y = _rms(x_ref[...]) * g_ref[...]
    o_ref[...] = (y * (1 + sc_ref[...]) + sh_ref[...]).astype(o_ref.dtype)


def _norm_mod_router_body(x_ref, g_ref, sc_ref, sh_ref, rw_hi_ref, rw_lo_ref, rb_ref,
                          of_ref, wt_ref, ix_ref):
    y = _rms(x_ref[...]) * g_ref[...]
    h = y * (1 + sc_ref[...]) + sh_ref[...]
    of_ref[...] = h
    h_hi, h_lo = _split_bf16(h)
    logits = _dot3(h_hi, h_lo, rw_hi_ref[...], rw_lo_ref[...]) + rb_ref[...]
    lane = lax.broadcasted_iota(jnp.int32, logits.shape, 1)
    lanef = lane.astype(F32)
    neg = jnp.float32(-jnp.inf)
    lg = jnp.where(lane < N_EXPERTS, logits, neg)
    m1 = jnp.max(lg, axis=-1, keepdims=True)
    i1 = jnp.min(jnp.where(lg == m1, lanef, float(LANES)), axis=-1, keepdims=True)
    lg2 = jnp.where(lanef == i1, neg, lg)
    m2 = jnp.max(lg2, axis=-1, keepdims=True)
    i2 = jnp.min(jnp.where(lg2 == m2, lanef, float(LANES)), axis=-1, keepdims=True)
    e = jnp.exp(m2 - m1)
    w1 = 1.0 / (1.0 + e)
    w2 = e / (1.0 + e)
    wt_ref[...] = jnp.where(lane == 0, w1, jnp.where(lane == 1, w2, 0.0))
    ix_ref[...] = jnp.where(lane == 0, i1, jnp.where(lane == 1, i2, 0.0)).astype(jnp.int32)


def _norm_mod(x, gain, layer, mods, k_scale, k_shift, router=None, bm=256):
    m = x.shape[0]
    in_specs = [
        pl.BlockSpec((bm, D_MODEL), lambda i: (i, 0)),
        pl.BlockSpec((None, 1, D_MODEL), lambda i: (layer, 0, 0)),
        pl.BlockSpec((None, 1, D_MODEL), lambda i: (_cond_of_tile(i, bm) * 6 + k_scale, 0, 0)),
        pl.BlockSpec((None, 1, D_MODEL), lambda i: (_cond_of_tile(i, bm) * 6 + k_shift, 0, 0)),
    ]
    args = [x, gain.reshape(DEPTH, 1, D_MODEL), mods, mods]
    row = pl.BlockSpec((bm, D_MODEL), lambda i: (i, 0))
    if router is None:
        return pl.pallas_call(
            _norm_mod_body, grid=(m // bm,), in_specs=in_specs, out_specs=row,
            out_shape=jax.ShapeDtypeStruct((m, D_MODEL), BF16),
            compiler_params=_params(1), name="norm_mod",
        )(*args)
    rw_hi, rw_lo, rb = router
    small = pl.BlockSpec((bm, LANES), lambda i: (i, 0))
    in_specs += [
        pl.BlockSpec((D_MODEL, LANES), lambda i: (0, 0)),
        pl.BlockSpec((D_MODEL, LANES), lambda i: (0, 0)),
        pl.BlockSpec((1, LANES), lambda i: (0, 0)),
    ]
    return pl.pallas_call(
        _norm_mod_router_body, grid=(m // bm,), in_specs=in_specs,
        out_specs=[row, small, small],
        out_shape=[jax.ShapeDtypeStruct((m, D_MODEL), F32),
                   jax.ShapeDtypeStruct((m, LANES), F32), jax.ShapeDtypeStruct((m, LANES), jnp.int32)],
        compiler_params=_params(1), name="norm_mod_router",
    )(*args, rw_hi, rw_lo, rb)


def _rmsnorm_body(x_ref, g_ref, o_ref):
    o_ref[...] = (_rms(x_ref[...]) * g_ref[...]).astype(o_ref.dtype)


def _rmsnorm(x, gain3, lead, width, col_blk, bm=512, row0=0, rows=None):
    m = rows or x.shape[0]
    return pl.pallas_call(
        _rmsnorm_body, grid=(m // bm,),
        in_specs=[pl.BlockSpec((bm, width), lambda i: (row0 // bm + i, col_blk)),
                  pl.BlockSpec((None, 1, width), lambda i: (lead, 0, 0))],
        out_specs=pl.BlockSpec((bm, width), lambda i: (i, 0)),
        out_shape=jax.ShapeDtypeStruct((m, width), F32),
        compiler_params=_params(1), name="rmsnorm",
    )(x, gain3)


def _dot_nt(a, b):
    return lax.dot_general(a, b, (((1,), (1,)), ((), ())), preferred_element_type=F32)


def _mm(name, x, ws, epilogue, *, bm, bn, n_out, out_dtype, k=None, x_kblk=0, extras=(),
        prologue=None, pro_extras=(), x_buffers=2):
    m = x.shape[0]
    k = k or x.shape[1]
    n_w, n_e, n_p = len(ws), len(extras), len(pro_extras)
    x_mode = {} if x_buffers == 2 else {"pipeline_mode": pl.Buffered(x_buffers)}
    in_specs = [pl.BlockSpec((bm, k), lambda i, j: (i, x_kblk), **x_mode)]
    args = [x]
    for arr, blk, imap in pro_extras:
        in_specs.append(pl.BlockSpec(blk, imap))
        args.append(arr)
    for arr, lead, col0, transposed in ws:
        squeeze = (None,) * len(lead)
        if transposed:
            in_specs.append(pl.BlockSpec((pl.Element(1),) * len(lead) + (pl.Element(bn), pl.Element(k)),
                                         lambda i, j, lead=lead, col0=col0:
                                         lead + (pl.multiple_of(col0 + j * bn, 8), 0)))
        else:
            assert col0 % bn == 0
            in_specs.append(pl.BlockSpec(squeeze + (k, bn),
                                         lambda i, j, lead=lead, col0=col0: lead + (0, j + col0 // bn)))
        args.append(arr)
    for arr, blk, imap in extras:
        in_specs.append(pl.BlockSpec(blk, imap))
        args.append(arr)

    def body(*refs):
        x_ref = refs[0]
        p_refs = refs[1:1 + n_p]
        w_refs = refs[1 + n_p:1 + n_p + n_w]
        e_refs = refs[1 + n_p + n_w:1 + n_p + n_w + n_e]
        o_ref = refs[1 + n_p + n_w + n_e]
        if prologue is None:
            xv = x_ref[...]
        else:
            xs_ref = refs[-1]

            @pl.when(pl.program_id(1) == 0)
            def _():
                xs_ref[...] = prologue(x_ref, *p_refs)

            xv = xs_ref[...]
        accs = []
        for (_, lead, _, transposed), w_ref in zip(ws, w_refs):
            if transposed:
                accs.append(_dot_nt(xv, w_ref[(0,) * len(lead)].astype(BF16)))
            else:
                accs.append(_dot(xv, w_ref[...].astype(BF16)))
        o_ref[...] = epilogue(accs, *e_refs).astype(o_ref.dtype)

    scratch = [] if prologue is None else [pltpu.VMEM((bm, k), BF16)]
    return pl.pallas_call(
        body, grid=(m // bm, pl.cdiv(n_out, bn)), in_specs=in_specs,
        out_specs=pl.BlockSpec((bm, bn), lambda i, j: (i, j)),
        out_shape=jax.ShapeDtypeStruct((m, n_out), out_dtype),
        scratch_shapes=scratch, compiler_params=_params(2), name=name,
    )(*args)


def _ep_plain(accs):
    return accs[0]


def _ep_swiglu(accs):
    a, b = accs
    return (a * jax.nn.sigmoid(a)) * b


def _ep_residual(accs, x_ref, g_ref):
    return x_ref[...] + g_ref[...] * accs[0]


def _ep_merge(accs, yf_ref, yc_ref, ya_ref):
    g0, g1, g2 = (jax.nn.sigmoid(a) for a in accs)
    return g0 * yf_ref[...] + g1 * yc_ref[...] + g2 * ya_ref[...]


def _rot_half_pairs(x):
    lane = lax.broadcasted_iota(jnp.int32, x.shape, 1)
    first = (lane // 16) % 2 == 0
    return jnp.where(first, -pltpu.roll(x, LANES - 16, 1), pltpu.roll(x, 16, 1))


def _ep_rope_q(accs, cos_ref, sin_ref):
    acc = accs[0]
    cos, sin = cos_ref[...], sin_ref[...]
    parts = []
    for h in range(acc.shape[1] // HEAD_PAD):
        base = h * HEAD_PAD
        parts.append(acc[:, base:base + QK_NOPE])
        pe = acc[:, base + QK_NOPE:base + HEAD_PAD]
        parts.append(pe * cos + _rot_half_pairs(pe) * sin)
    return jnp.concatenate(parts, axis=1)


def _pro_rmsnorm(x_ref, g_ref):
    return (_rms(x_ref[...]) * g_ref[...]).astype(BF16)


def _dft_tables(n, sign):
    p = jnp.arange(n, dtype=jnp.int32)
    ang = ((p[:, None] * p[None, :]) % n).astype(F32) * np.float32(2.0 * np.pi / n)
    scale = np.float32(1.0 / np.sqrt(n))
    return _split_bf16(jnp.cos(ang) * scale) + _split_bf16(jnp.sin(ang) * (sign * scale))


def _fourier_chan_body(u_ref, c_hi, c_lo, s_hi, s_lo, a_hi, a_lo, b_hi, b_lo):
    for g in range(N_FOURIER_GROUPS):
        sl = slice(g * FOURIER_GROUP, (g + 1) * FOURIER_GROUP)
        u_h, u_l = _split_bf16(u_ref[:, sl])
        ah, al = _split_bf16(_dot3(u_h, u_l, c_hi[...], c_lo[...]))
        bh, bl = _split_bf16(_dot3(u_h, u_l, s_hi[...], s_lo[...]))
        a_hi[:, sl] = ah
        a_lo[:, sl] = al
        b_hi[:, sl] = bh
        b_lo[:, sl] = bl


def _fourier_chan(z, tabs, bm=512):
    m = z.shape[0]
    tab = pl.BlockSpec((FOURIER_GROUP, FOURIER_GROUP), lambda i: (0, 0))
    row = pl.BlockSpec((bm, D_FOURIER), lambda i: (i, 0))
    return pl.pallas_call(
        _fourier_chan_body, grid=(m // bm,),
        in_specs=[row, tab, tab, tab, tab], out_specs=[row] * 4,
        out_shape=[jax.ShapeDtypeStruct((m, D_FOURIER), BF16)] * 4,
        compiler_params=_params(1), name="fourier_chan",
    )(z, *tabs)


def _fourier_seq_body(c_hi, c_lo, s_hi, s_lo, a_hi, a_lo, b_hi, b_lo, o_ref):
    y = _dot3(c_hi[...], c_lo[...], a_hi[...], a_lo[...]) + _dot3(s_hi[...], s_lo[...], b_hi[...], b_lo[...])
    o_ref[...] = y.astype(o_ref.dtype)


def _fourier_seq(ab, tabs, n_seq, seq_len, seq_blk0, bl, bn):
    nr, nc = seq_len // bl, D_FOURIER // bn
    tab = pl.BlockSpec((bl, seq_len), lambda s, c, r: (r, 0))
    dat = pl.BlockSpec((seq_len, bn), lambda s, c, r: (seq_blk0 + s, c))
    return pl.pallas_call(
        _fourier_seq_body, grid=(n_seq, nc, nr),
        in_specs=[tab] * 4 + [dat] * 4,
        out_specs=pl.BlockSpec((bl, bn), lambda s, c, r: (s * nr + r, c)),
        out_shape=jax.ShapeDtypeStruct((n_seq * seq_len, D_FOURIER), BF16),
        compiler_params=_params(3), name=f"fourier_seq{seq_len}",
    )(*tabs, *ab)


CONV_ROWS = 2048


def _conv_body(b_ref, c_ref, h_ref, w_ref, o_ref):
    i = pl.program_id(0)
    seq_len = jnp.where(i * CONV_ROWS < N_PROMPT, SEQ, DEC_SEQ)
    u = c_ref[...] * h_ref[...]
    pos = lax.broadcasted_iota(jnp.int32, u.shape, 0) & (seq_len - 1)
    prev = jnp.where(pos == 0, 0.0, pltpu.roll(u, 1, 0))
    nxt = jnp.where(pos == seq_len - 1, 0.0, pltpu.roll(u, CONV_ROWS - 1, 0))
    w = w_ref[...]
    o_ref[...] = (b_ref[...] * (w[0:1] * prev + w[1:2] * u + w[2:3] * nxt)).astype(o_ref.dtype)


def _conv(z, conv_w, layer, bc=256):
    m = z.shape[0]
    nb = D_CONV // bc
    col0 = D_FOURIER // bc

    def part(k):
        return pl.BlockSpec((CONV_ROWS, bc), lambda i, j: (i, col0 + k * nb + j))

    return pl.pallas_call(
        _conv_body, grid=(m // CONV_ROWS, nb),
        in_specs=[part(0), part(1), part(2), pl.BlockSpec((None, 3, bc), lambda i, j: (layer, 0, j))],
        out_specs=pl.BlockSpec((CONV_ROWS, bc), lambda i, j: (i, j)),
        out_shape=jax.ShapeDtypeStruct((m, D_CONV), BF16),
        compiler_params=_params(2), name="short_conv",
    )(z, z, z, conv_w)


def _rope_k_body(x_ref, cos_ref, sin_ref, o_ref):
    x = x_ref[...]
    o_ref[...] = (x * cos_ref[...] + _rot_half_pairs(x) * sin_ref[...]).astype(o_ref.dtype)


def _rope_k(x, cos, sin, bm=512):
    m = x.shape[0]
    blk = pl.BlockSpec((bm, LANES), lambda i: (i, 0))
    return pl.pallas_call(
        _rope_k_body, grid=(m // bm,), in_specs=[blk, blk, blk], out_specs=blk,
        out_shape=jax.ShapeDtypeStruct((m, LANES), BF16),
        compiler_params=_params(1), name="rope_k",
    )(x, cos, sin)


def _attn_body(q_ref, kv_ref, kp_ref, o_ref, kcat_ref, *, heads):
    @pl.when(pl.program_id(2) == 0)
    def _():
        kp = kp_ref[...]
        for h in range(heads):
            kcat_ref[h, :, :QK_NOPE] = kv_ref[:, h * HEAD_PAD:h * HEAD_PAD + QK_NOPE]
            kcat_ref[h, :, QK_NOPE:] = kp

    for h in range(heads):
        q = q_ref[:, h * HEAD_PAD:(h + 1) * HEAD_PAD]
        v = kv_ref[:, h * HEAD_PAD + QK_NOPE:(h + 1) * HEAD_PAD]
        s = _dot_nt(q, kcat_ref[h]) * np.float32((QK_NOPE + QK_ROPE) ** -0.5)
        p = jnp.exp(s - jnp.max(s, axis=-1, keepdims=True))
        l = jnp.sum(p, axis=-1, keepdims=True)
        o_ref[:, h * V_HEAD:(h + 1) * V_HEAD] = (_dot(p.astype(BF16), v) / l).astype(o_ref.dtype)


def _attention(q, kv, kpe, n_seq, lq, lk, q_row0, k_row0, bq, heads):
    nq = lq // bq
    return pl.pallas_call(
        functools.partial(_attn_body, heads=heads), grid=(n_seq, N_HEADS // heads, nq),
        in_specs=[
            pl.BlockSpec((bq, heads * HEAD_PAD), lambda b, h, i: (q_row0 // bq + b * nq + i, h)),
            pl.BlockSpec((lk, heads * (QK_NOPE + V_HEAD)), lambda b, h, i: (k_row0 // lk + b, h)),
            pl.BlockSpec((lk, LANES), lambda b, h, i: (k_row0 // lk + b, 0)),
        ],
        out_specs=pl.BlockSpec((bq, heads * V_HEAD), lambda b, h, i: (b * nq + i, h)),
        out_shape=jax.ShapeDtypeStruct((n_seq * lq, N_HEADS * V_HEAD), BF16),
        scratch_shapes=[pltpu.VMEM((heads, lk, HEAD_PAD), BF16)],
        compiler_params=_params(3), name=f"attention{lq}",
    )(q, kv, kpe)


def _route(idx):
    flat_e = idx.reshape(-1)
    n = flat_e.shape[0]
    onehot = (flat_e[:, None] == jnp.arange(N_EXPERTS, dtype=jnp.int32)[None, :]).astype(jnp.int32)
    csum = jnp.cumsum(onehot, axis=0)
    rank = jnp.take_along_axis(csum, flat_e[:, None], axis=1)[:, 0] - 1
    counts = csum[-1]
    n_tile = (counts + MOE_BM - 1) // MOE_BM
    tile_end = jnp.cumsum(n_tile)
    tile_start = tile_end - n_tile
    pos = tile_start[flat_e] * MOE_BM + rank
    n_valid = tile_end[-1]
    t = jnp.arange(MOE_TILES, dtype=jnp.int32)
    tc = jnp.minimum(t, n_valid - 1)
    tile_e = jnp.sum((tile_end[None, :] <= tc[:, None]).astype(jnp.int32), axis=1)
    rows = jnp.clip(counts[tile_e] - (tc - tile_start[tile_e]) * MOE_BM, 0, MOE_BM)
    tile_rows = jnp.where(t < n_valid, rows, 0)
    src = jnp.zeros((MOE_TILES * MOE_BM,), jnp.int32).at[pos].set(jnp.arange(n, dtype=jnp.int32) // TOP_K)
    i32 = jnp.int32
    return pos.astype(i32), src, tile_e.astype(i32), tile_rows.astype(i32), n_valid.reshape(1).astype(i32)


GATHER_UNROLL = 8


def _gather_body(src_ref, nr_ref, h_hbm, o_ref, buf, sem):
    t, s = pl.program_id(0), pl.program_id(1)
    base = t * MOE_BM + s * GATHER_ROWS
    active = s * GATHER_ROWS < nr_ref[t]

    def row_copy(r, src_row):
        return pltpu.make_async_copy(h_hbm.at[pl.ds(src_row, 1)], buf.at[pl.ds(r, 1)], sem)

    @pl.when(active)
    def _():
        def start(g, c):
            for u in range(GATHER_UNROLL):
                r = g * GATHER_UNROLL + u
                row_copy(r, src_ref[base + r]).start()
            return c

        def wait(g, c):
            for u in range(GATHER_UNROLL):
                row_copy(g * GATHER_UNROLL + u, 0).wait()
            return c

        lax.fori_loop(0, GATHER_ROWS // GATHER_UNROLL, start, 0)
        lax.fori_loop(0, GATHER_ROWS // GATHER_UNROLL, wait, 0)
        o_ref[...] = buf[...].astype(o_ref.dtype)

    @pl.when(jnp.logical_not(active))
    def _():
        o_ref[...] = jnp.zeros_like(o_ref)


def _gather_rows(h, src, tile_rows):
    n_sub = MOE_BM // GATHER_ROWS
    grid_spec = pltpu.PrefetchScalarGridSpec(
        num_scalar_prefetch=2, grid=(MOE_TILES, n_sub),
        in_specs=[pl.BlockSpec(memory_space=pl.ANY)],
        out_specs=pl.BlockSpec((GATHER_ROWS, D_MODEL), lambda t, s, src, nr: (t * n_sub + s, 0)),
        scratch_shapes=[pltpu.VMEM((GATHER_ROWS, D_MODEL), F32), pltpu.SemaphoreType.DMA(())],
    )
    return pl.pallas_call(
        _gather_body, grid_spec=grid_spec,
        out_shape=jax.ShapeDtypeStruct((MOE_TILES * MOE_BM, D_MODEL), BF16),
        compiler_params=_params(2), name="moe_gather",
    )(src, tile_rows, h)


def _sub_blocks(nr_ref):
    return (nr_ref[pl.program_id(0)] + MOE_SUB - 1) // MOE_SUB


def _moe_up_body(te_ref, nr_ref, nv_ref, x_ref, w1_ref, w3_ref, o_ref):
    n_sub = _sub_blocks(nr_ref)
    for n in range(MOE_BM // MOE_SUB + 1):
        @pl.when(n_sub == n)
        def _(n=n):
            m = n * MOE_SUB
            if m:
                xv = x_ref[:m, :]
                a = _dot(xv, w1_ref[...].astype(BF16))
                b = _dot(xv, w3_ref[...].astype(BF16))
                o_ref[:m, :] = ((a * jax.nn.sigmoid(a)) * b).astype(o_ref.dtype)
            if m < MOE_BM:
                o_ref[m:, :] = jnp.zeros((MOE_BM - m, o_ref.shape[1]), o_ref.dtype)


def _moe_up(xs, w1, w3, moe_layer, tile_e, tile_rows, n_valid, bn=256):
    gj = D_FF_EXPERT // bn

    def row(t, nv):
        return jnp.minimum(t, nv[0] - 1)

    def col(t, j, nv):
        return jnp.where(t < nv[0], j, gj - 1)

    w_spec = pl.BlockSpec((None, None, D_MODEL, bn),
                          lambda t, j, te, nr, nv: (moe_layer, te[t], 0, col(t, j, nv)))
    grid_spec = pltpu.PrefetchScalarGridSpec(
        num_scalar_prefetch=3, grid=(MOE_TILES, gj),
        in_specs=[pl.BlockSpec((MOE_BM, D_MODEL), lambda t, j, te, nr, nv: (row(t, nv), 0)), w_spec, w_spec],
        out_specs=pl.BlockSpec((MOE_BM, bn), lambda t, j, te, nr, nv: (t, j)),
    )
    return pl.pallas_call(
        _moe_up_body, grid_spec=grid_spec,
        out_shape=jax.ShapeDtypeStruct((xs.shape[0], D_FF_EXPERT), BF16),
        compiler_params=_params(2), name="moe_up",
    )(tile_e, tile_rows, n_valid, xs, w1, w3)


def _moe_down_body(te_ref, nr_ref, nv_ref, x_ref, w_ref, o_ref):
    n_sub = _sub_blocks(nr_ref)
    first = pl.program_id(2) == 0
    for n in range(MOE_BM // MOE_SUB + 1):
        m = n * MOE_SUB

        @pl.when((n_sub == n) & first)
        def _(m=m):
            if m:
                o_ref[:m, :] = _dot(x_ref[:m, :], w_ref[...].astype(BF16))
            if m < MOE_BM:
                o_ref[m:, :] = jnp.zeros((MOE_BM - m, o_ref.shape[1]), o_ref.dtype)

        if m:
            @pl.when((n_sub == n) & jnp.logical_not(first))
            def _(m=m):
                o_ref[:m, :] += _dot(x_ref[:m, :], w_ref[...].astype(BF16))


def _moe_down(mid, w2, moe_layer, tile_e, tile_rows, n_valid, bn=1024, bk=2048):
    gj, gk = D_MODEL // bn, D_FF_EXPERT // bk

    def row(t, nv):
        return jnp.minimum(t, nv[0] - 1)

    def clamp(t, v, last, nv):
        return jnp.where(t < nv[0], v, last)

    grid_spec = pltpu.PrefetchScalarGridSpec(
        num_scalar_prefetch=3, grid=(MOE_TILES, gj, gk),
        in_specs=[
            pl.BlockSpec((MOE_BM, bk), lambda t, j, k, te, nr, nv: (row(t, nv), clamp(t, k, gk - 1, nv))),
            pl.BlockSpec((None, None, bk, bn),
                         lambda t, j, k, te, nr, nv: (moe_layer, te[t], clamp(t, k, gk - 1, nv),
                                                      clamp(t, j, gj - 1, nv))),
        ],
        out_specs=pl.BlockSpec((MOE_BM, bn), lambda t, j, k, te, nr, nv: (t, j)),
    )
    return pl.pallas_call(
        _moe_down_body, grid_spec=grid_spec,
        out_shape=jax.ShapeDtypeStruct((mid.shape[0], D_MODEL), F32),
        compiler_params=_params(3), name="moe_down",
    )(tile_e, tile_rows, n_valid, mid, w2)


def _combine_body(pos_ref, y_hbm, wt_ref, x_ref, g_ref, o_ref, buf, sem):
    base = pl.program_id(0) * GATHER_ROWS

    def row_copy(r, c, src_row):
        return pltpu.make_async_copy(y_hbm.at[pl.ds(src_row, 1)], buf.at[c, pl.ds(r, 1)], sem)

    def start(g, carry):
        for u in range(GATHER_UNROLL):
            r = g * GATHER_UNROLL + u
            for c in range(TOP_K):
                row_copy(r, c, pos_ref[(base + r) * TOP_K + c]).start()
        return carry

    def wait(g, carry):
        for u in range(GATHER_UNROLL):
            for c in range(TOP_K):
                row_copy(g * GATHER_UNROLL + u, c, 0).wait()
        return carry

    lax.fori_loop(0, GATHER_ROWS // GATHER_UNROLL, start, 0)
    lax.fori_loop(0, GATHER_ROWS // GATHER_UNROLL, wait, 0)
    wt = wt_ref[...]
    f = wt[:, 0:1] * buf[0] + wt[:, 1:2] * buf[1]
    o_ref[...] = x_ref[...] + g_ref[...] * f


def _moe_combine(y, pos, wts, x, mods, k_gate):
    m = x.shape[0]
    row = pl.BlockSpec((GATHER_ROWS, D_MODEL), lambda i, p: (i, 0))
    grid_spec = pltpu.PrefetchScalarGridSpec(
        num_scalar_prefetch=1, grid=(m // GATHER_ROWS,),
        in_specs=[
            pl.BlockSpec(memory_space=pl.ANY),
            pl.BlockSpec((GATHER_ROWS, LANES), lambda i, p: (i, 0)),
            row,
            pl.BlockSpec((None, 1, D_MODEL), lambda i, p: (_cond_of_tile(i, GATHER_ROWS) * 6 + k_gate, 0, 0)),
        ],
        out_specs=row,
        scratch_shapes=[pltpu.VMEM((TOP_K, GATHER_ROWS, D_MODEL), F32), pltpu.SemaphoreType.DMA(())],
    )
    return pl.pallas_call(
        _combine_body, grid_spec=grid_spec,
        out_shape=jax.ShapeDtypeStruct((m, D_MODEL), F32),
        compiler_params=_params(1), name="moe_combine",
    )(pos, y, wts, x, mods)


def _rope_tables():
    n_rows = DEC_SEQ // GRID_W
    row = jnp.repeat(jnp.arange(n_rows), GRID_W).astype(F32)
    col = jnp.tile(jnp.arange(GRID_W), n_rows).astype(F32)
    half = QK_ROPE // 2
    inv = ROPE_BASE ** (-(jnp.arange(half // 2, dtype=F32) * 2.0) / half)
    ang_r = row[:, None] * inv
    ang_c = col[:, None] * inv
    ang = jnp.concatenate([ang_r, ang_r, ang_c, ang_c], axis=-1)
    pad1 = jnp.ones((DEC_SEQ, LANES - QK_ROPE), F32)
    cos = jnp.concatenate([jnp.cos(ang), pad1], axis=1)
    sin = jnp.concatenate([jnp.sin(ang), 0 * pad1], axis=1)
    return cos, sin


def _mod_spec(bm, bn, which):
    return ((None, 1, bn), lambda i, j: (_cond_of_tile(i, bm) * 6 + which, 0, j))


def kernel(x_prompt, x_sample, cache_ckv, cache_kpe, c, c_ctx, norm1_g, norm2_g, w_ada, b_ada, w_in,
           w_fourier_out, conv_w, w_conv_out, q_norm_g, w_uq, kv_norm_g, w_ukv, w_mla_out, w_o,
           ffn_w1, ffn_w3, ffn_w2, router_w, router_b, moe_w1, moe_w3, moe_w2, final_g):
    bm = 1024
    x = jnp.concatenate([x_prompt.reshape(N_PROMPT, D_MODEL), x_sample.reshape(N_SAMPLE, D_MODEL)], axis=0)
    w_in_t = jnp.swapaxes(w_in, 1, 2)

    cond8 = jnp.concatenate([c_ctx[None, :], c, jnp.zeros((8 - 1 - DEC_BATCH, D_MODEL), F32)], axis=0)
    mods_all = _ada(cond8, w_ada, b_ada)[:, :1 + DEC_BATCH].reshape(DEPTH, (1 + DEC_BATCH) * 6, 1, D_MODEL)

    cos, sin = _rope_tables()
    one = jnp.ones((N_PROMPT, LANES), F32)
    cos_q = jnp.concatenate([one] + [cos] * DEC_BATCH, axis=0)
    sin_q = jnp.concatenate([0 * one] + [sin] * DEC_BATCH, axis=0)
    one_p = jnp.ones((PAST_LEN, LANES), F32)
    cos_k = jnp.concatenate([one_p, cos] * DEC_BATCH + [one], axis=0)
    sin_k = jnp.concatenate([0 * one_p, sin] * DEC_BATCH + [0 * one], axis=0)
    tab_chan = _dft_tables(FOURIER_GROUP, 1.0)
    tab_prompt = _dft_tables(SEQ, -1.0)
    tab_sample = _dft_tables(DEC_SEQ, -1.0)

    ckv_out, kpe_out = [], []
    for layer in range(DEPTH):
        mods = mods_all[layer]
        uq = w_uq[layer].reshape(Q_LORA, N_HEADS, QK_NOPE + QK_ROPE)
        uq = jnp.concatenate([uq, jnp.zeros((Q_LORA, N_HEADS, HEAD_PAD - QK_NOPE - QK_ROPE), F32)], axis=-1)
        uq = uq.reshape(Q_LORA, N_HEADS * HEAD_PAD).astype(BF16)

        h = _norm_mod(x, norm1_g, layer, mods, 1, 0)
        z = _mm("in_proj", h, [(w_in_t, (layer,), 0, True)], _ep_plain, bm=bm, bn=512, n_out=Z_MAIN, out_dtype=F32)

        ab = _fourier_chan(z, tab_chan)
        yf_pre = jnp.concatenate([
            _fourier_seq(ab, tab_prompt, BATCH, SEQ, 0, SEQ, D_FOURIER),
            _fourier_seq(ab, tab_sample, DEC_BATCH, DEC_SEQ, N_PROMPT // DEC_SEQ, 512, 512)], axis=0)
        y_f = _mm("fourier_out", yf_pre, [(w_fourier_out, (layer,), 0, False)], _ep_plain,
                  bm=bm, bn=2048, n_out=D_MODEL, out_dtype=F32)

        yc_pre = _conv(z, conv_w, layer)
        y_c = _mm("conv_out", yc_pre, [(w_conv_out, (layer,), 0, False)], _ep_plain,
                  bm=bm, bn=2048, n_out=D_MODEL, out_dtype=F32)

        q_col = (D_FOURIER + 3 * D_CONV) // Q_LORA
        q = _mm("q_proj", z, [(uq, (), 0, False)], _ep_rope_q, bm=bm, bn=512, n_out=N_HEADS * HEAD_PAD,
                out_dtype=BF16, k=Q_LORA, x_kblk=q_col, prologue=_pro_rmsnorm,
                pro_extras=[(q_norm_g.reshape(DEPTH, 1, Q_LORA), (None, 1, Q_LORA), lambda i, j: (layer, 0, 0))],
                extras=[(cos_q, (bm, LANES), lambda i, j: (i, 0)), (sin_q, (bm, LANES), lambda i, j: (i, 0))])
        kv_col = (D_FOURIER + 3 * D_CONV + Q_LORA) // KV_LORA
        c_kv = _rmsnorm(z, kv_norm_g.reshape(DEPTH, 1, KV_LORA), layer, KV_LORA, kv_col)
        k_pe = z[:, GATE_COL0 - QK_ROPE:GATE_COL0]
        ckv_out.append(c_kv[:N_PROMPT].reshape(BATCH, SEQ, KV_LORA))
        kpe_out.append(k_pe[:N_PROMPT].reshape(BATCH, SEQ, QK_ROPE))
        ckv_s = jnp.concatenate([cache_ckv[:, layer], c_kv[N_PROMPT:].reshape(DEC_BATCH, DEC_SEQ, KV_LORA)], axis=1)
        ckv_ext = jnp.concatenate([ckv_s.reshape(-1, KV_LORA), c_kv[:N_PROMPT]], axis=0).astype(BF16)
        kpe_s = jnp.concatenate([cache_kpe[:, layer], k_pe[N_PROMPT:].reshape(DEC_BATCH, DEC_SEQ, QK_ROPE)], axis=1)
        kpe_ext = jnp.concatenate([kpe_s.reshape(-1, QK_ROPE), k_pe[:N_PROMPT]], axis=0)
        kpe_ext = jnp.concatenate([kpe_ext, jnp.zeros((N_KV_ROWS, LANES - QK_ROPE), F32)], axis=1)
        kpe_ext = _rope_k(kpe_ext, cos_k, sin_k)
        kv = _mm("kv_proj", ckv_ext, [(w_ukv, (layer,), 0, False)], _ep_plain,
                 bm=512, bn=2048, n_out=N_HEADS * (QK_NOPE + V_HEAD), out_dtype=BF16)
        attn = jnp.concatenate([
            _attention(q, kv, kpe_ext, BATCH, SEQ, SEQ, 0, DEC_BATCH * KV_LEN, SEQ, N_HEADS),
            _attention(q, kv, kpe_ext, DEC_BATCH, DEC_SEQ, KV_LEN, N_PROMPT, 0, 512, 8)], axis=0)
        y_a = _mm("mla_out", attn, [(w_mla_out, (layer,), 0, False)], _ep_plain,
                  bm=bm, bn=1024, n_out=D_MODEL, out_dtype=F32)

        tile = ((bm, 256), lambda i, j: (i, j))
        merged = _mm("gate_merge", h,
                     [(w_in_t, (layer,), GATE_COL0 + b * D_MODEL, True) for b in range(N_BRANCH)], _ep_merge,
                     bm=bm, bn=256, n_out=D_MODEL, out_dtype=BF16, x_buffers=1,
                     extras=[(y_f,) + tile, (y_c,) + tile, (y_a,) + tile])
        x = _mm("out_proj", merged, [(w_o, (layer,), 0, False)], _ep_residual, bm=bm, bn=512, n_out=D_MODEL,
                out_dtype=F32, extras=[(x, (bm, 512), lambda i, j: (i, j)), (mods,) + _mod_spec(bm, 512, 2)])

        j = layer // 2
        if layer % 2 == 0:
            h2 = _norm_mod(x, norm2_g, layer, mods, 4, 3)
            mid = _mm("ffn_up", h2, [(ffn_w1, (j,), 0, False), (ffn_w3, (j,), 0, False)], _ep_swiglu,
                      bm=bm, bn=256, n_out=D_FF, out_dtype=BF16)
            x = _mm("ffn_down", mid, [(ffn_w2[j].astype(BF16), (), 0, False)], _ep_residual, bm=512, bn=256,
                    n_out=D_MODEL, out_dtype=F32,
                    extras=[(x, (512, 256), lambda i, j: (i, j)), (mods,) + _mod_spec(512, 256, 5)])
        else:
            rw = jnp.concatenate([router_w[j], jnp.zeros((D_MODEL, LANES - N_EXPERTS), F32)], axis=1)
            rb = jnp.concatenate([router_b[j], jnp.zeros((LANES - N_EXPERTS,), F32)])[None, :]
            h2f, wts, idx = _norm_mod(x, norm2_g, layer, mods, 4, 3, router=_split_bf16(rw) + (rb,))
            pos, src, tile_e, tile_rows, n_valid = _route(idx[:, :TOP_K])
            xs = _gather_rows(h2f, src, tile_rows)
            mid = _moe_up(xs, moe_w1, moe_w3, j, tile_e, tile_rows, n_valid)
            y = _moe_down(mid, moe_w2, j, tile_e, tile_rows, n_valid)
            x = _moe_combine(y, pos, wts, x, mods, 5)

    fg = final_g.reshape(1, 1, D_MODEL)
    y_prompt = _rmsnorm(x, fg, 0, D_MODEL, 0, bm=256, row0=0, rows=N_PROMPT)
    y_sample = _rmsnorm(x, fg, 0, D_MODEL, 0, bm=256, row0=N_PROMPT, rows=N_SAMPLE)
    return (y_prompt.reshape(BATCH, SEQ, D_MODEL),
            y_sample.reshape(DEC_BATCH, DEC_SEQ, D_MODEL),
            jnp.stack(ckv_out, axis=1),
            jnp.stack(kpe_out, axis=1))
```

```python
import functools

import jax
import jax.numpy as jnp
import numpy as np
from jax import lax
from jax.experimental import pallas as pl
from jax.experimental.pallas import tpu as pltpu

F32 = jnp.float32
BF16 = jnp.bfloat16

D_MODEL = 4096
BATCH = 16
SEQ = 256
DEPTH = 2
DEC_BATCH = 2
DEC_SEQ = 2048
PAST_LEN = 256
GRID_W = 64
N_FOURIER_GROUPS = 4
FOURIER_GROUP = 256
D_FOURIER = N_FOURIER_GROUPS * FOURIER_GROUP
D_CONV = 1024
N_HEADS = 16
Q_LORA = 1024
KV_LORA = 512
QK_NOPE = 128
QK_ROPE = 64
V_HEAD = 128
ROPE_BASE = 10000.0
N_BRANCH = 3
D_FF = 11008
N_EXPERTS = 8
TOP_K = 2
D_FF_EXPERT = 14336
EPS = 1e-6

N_PROMPT = BATCH * SEQ
N_SAMPLE = DEC_BATCH * DEC_SEQ
N_TOK = N_PROMPT + N_SAMPLE
KV_LEN = PAST_LEN + DEC_SEQ
N_KV_ROWS = DEC_BATCH * KV_LEN + N_PROMPT
GATE_COL0 = D_FOURIER + 3 * D_CONV + Q_LORA + KV_LORA + QK_ROPE
Z_MAIN = 6144
HEAD_PAD = 256
LANES = 128

VMEM_LIMIT = 56 * 1024 * 1024
MOE_SUB = 256
MOE_BM = 5 * MOE_SUB
MOE_TILES = (TOP_K * N_TOK) // MOE_BM + N_EXPERTS
GATHER_ROWS = MOE_BM // 2
COMBINE_ROWS = 512


def _params(n_axes, vmem=VMEM_LIMIT):
    return pltpu.CompilerParams(dimension_semantics=("arbitrary",) * n_axes, vmem_limit_bytes=vmem)


def _cond_of_tile(i, bm):
    r0 = i * bm
    return jnp.where(r0 < N_PROMPT, 0, 1 + (r0 - N_PROMPT) // DEC_SEQ)


def _split_bf16(x):
    hi = x.astype(BF16)
    lo = (x - hi.astype(F32)).astype(BF16)
    return hi, lo


def _dot(a, b):
    return jnp.dot(a, b, preferred_element_type=F32)


def _dot3(a_hi, a_lo, b_hi, b_lo):
    return _dot(a_hi, b_hi) + _dot(a_hi, b_lo) + _dot(a_lo, b_hi)


def _ada_body(c_ref, w_ref, b_ref, o_ref):
    c = c_ref[...]
    s = (c * jax.nn.sigmoid(c)).astype(BF16)
    o_ref[...] = _dot(s, w_ref[...].astype(BF16)) + b_ref[...]


def _ada(cond8, w_ada, b_ada, bn=512):
    n = w_ada.shape[-1]
    return pl.pallas_call(
        _ada_body,
        grid=(DEPTH, n // bn),
        in_specs=[
            pl.BlockSpec((8, D_MODEL), lambda l, j: (0, 0)),
            pl.BlockSpec((None, D_MODEL, bn), lambda l, j: (l, 0, j)),
            pl.BlockSpec((None, 1, bn), lambda l, j: (l, 0, j)),
        ],
        out_specs=pl.BlockSpec((None, 8, bn), lambda l, j: (l, 0, j)),
        out_shape=jax.ShapeDtypeStruct((DEPTH, 8, n), F32),
        compiler_params=_params(2),
        name="ada",
    )(cond8, w_ada, b_ada.reshape(DEPTH, 1, n))


def _rms(x):
    return x * lax.rsqrt(jnp.mean(x * x, axis=-1, keepdims=True) + EPS)


def _norm_mod_body(x_ref, g_ref, sc_ref, sh_ref, o_ref):
    y = _rms(x_ref[...]) * g_ref[...]
    o_ref[...] = (y * (1 + sc_ref[...]) + sh_ref[...]).astype(o_ref.dtype)


def _norm_mod_router_body(x_ref, g_ref, sc_ref, sh_ref, rw_hi_ref, rw_lo_ref, rb_ref,
                          of_ref, wt_ref, ix_ref):
    y = _rms(x_ref[...]) * g_ref[...]
    h = y * (1 + sc_ref[...]) + sh_ref[...]
    of_ref[...] = h
    h_hi, h_lo = _split_bf16(h)
    logits = _dot3(h_hi, h_lo, rw_hi_ref[...], rw_lo_ref[...]) + rb_ref[...]
    lane = lax.broadcasted_iota(jnp.int32, logits.shape, 1)
    lanef = lane.astype(F32)
    neg = jnp.float32(-jnp.inf)
    lg = jnp.where(lane < N_EXPERTS, logits, neg)
    m1 = jnp.max(lg, axis=-1, keepdims=True)
    i1 = jnp.min(jnp.where(lg == m1, lanef, float(LANES)), axis=-1, keepdims=True)
    lg2 = jnp.where(lanef == i1, neg, lg)
    m2 = jnp.max(lg2, axis=-1, keepdims=True)
    i2 = jnp.min(jnp.where(lg2 == m2, lanef, float(LANES)), axis=-1, keepdims=True)
    e = jnp.exp(m2 - m1)
    w1 = 1.0 / (1.0 + e)
    w2 = e / (1.0 + e)
    wt_ref[...] = jnp.where(lane == 0, w1, jnp.where(lane == 1, w2, 0.0))
    ix_ref[...] = jnp.where(lane == 0, i1, jnp.where(lane == 1, i2, 0.0)).astype(jnp.int32)


def _norm_mod(x, gain, layer, mods, k_scale, k_shift, router=None, bm=256):
    m = x.shape[0]
    in_specs = [
        pl.BlockSpec((bm, D_MODEL), lambda i: (i, 0)),
        pl.BlockSpec((None, 1, D_MODEL), lambda i: (layer, 0, 0)),
        pl.BlockSpec((None, 1, D_MODEL), lambda i: (_cond_of_tile(i, bm) * 6 + k_scale, 0, 0)),
        pl.BlockSpec((None, 1, D_MODEL), lambda i: (_cond_of_tile(i, bm) * 6 + k_shift, 0, 0)),
    ]
    args = [x, gain.reshape(DEPTH, 1, D_MODEL), mods, mods]
    row = pl.BlockSpec((bm, D_MODEL), lambda i: (i, 0))
    if router is None:
        return pl.pallas_call(
            _norm_mod_body, grid=(m // bm,), in_specs=in_specs, out_specs=row,
            out_shape=jax.ShapeDtypeStruct((m, D_MODEL), BF16),
            compiler_params=_params(1), name="norm_mod",
        )(*args)
    rw_hi, rw_lo, rb = router
    small = pl.BlockSpec((bm, LANES), lambda i: (i, 0))
    in_specs += [
        pl.BlockSpec((D_MODEL, LANES), lambda i: (0, 0)),
        pl.BlockSpec((D_MODEL, LANES), lambda i: (0, 0)),
        pl.BlockSpec((1, LANES), lambda i: (0, 0)),
    ]
    return pl.pallas_call(
        _norm_mod_router_body, grid=(m // bm,), in_specs=in_specs,
        out_specs=[row, small, small],
        out_shape=[jax.ShapeDtypeStruct((m, D_MODEL), F32),
                   jax.ShapeDtypeStruct((m, LANES), F32), jax.ShapeDtypeStruct((m, LANES), jnp.int32)],
        compiler_params=_params(1), name="norm_mod_router",
    )(*args, rw_hi, rw_lo, rb)


def _rmsnorm_body(x_ref, g_ref, o_ref):
    o_ref[...] = (_rms(x_ref[...]) * g_ref[...]).astype(o_ref.dtype)


def _rmsnorm(x, gain3, lead, width, col_blk, bm=512, row0=0, rows=None):
    m = rows or x.shape[0]
    return pl.pallas_call(
        _rmsnorm_body, grid=(m // bm,),
        in_specs=[pl.BlockSpec((bm, width), lambda i: (row0 // bm + i, col_blk)),
                  pl.BlockSpec((None, 1, width), lambda i: (lead, 0, 0))],
        out_specs=pl.BlockSpec((bm, width), lambda i: (i, 0)),
        out_shape=jax.ShapeDtypeStruct((m, width), F32),
        compiler_params=_params(1), name="rmsnorm",
    )(x, gain3)


def _dot_nt(a, b):
    return lax.dot_general(a, b, (((1,), (1,)), ((), ())), preferred_element_type=F32)


def _mm(name, x, ws, epilogue, *, bm, bn, n_out, out_dtype, k=None, x_kblk=0, extras=(),
        prologue=None, pro_extras=(), x_buffers=2):
    m = x.shape[0]
    k = k or x.shape[1]
    n_w, n_e, n_p = len(ws), len(extras), len(pro_extras)
    x_mode = {} if x_buffers == 2 else {"pipeline_mode": pl.Buffered(x_buffers)}
    in_specs = [pl.BlockSpec((bm, k), lambda i, j: (i, x_kblk), **x_mode)]
    args = [x]
    for arr, blk, imap in pro_extras:
        in_specs.append(pl.BlockSpec(blk, imap))
        args.append(arr)
    for arr, lead, col0, transposed in ws:
        squeeze = (None,) * len(lead)
        if transposed:
            in_specs.append(pl.BlockSpec((pl.Element(1),) * len(lead) + (pl.Element(bn), pl.Element(k)),
                                         lambda i, j, lead=lead, col0=col0:
                                         lead + (pl.multiple_of(col0 + j * bn, 8), 0)))
        else:
            assert col0 % bn == 0
            in_specs.append(pl.BlockSpec(squeeze + (k, bn),
                                         lambda i, j, lead=lead, col0=col0: lead + (0, j + col0 // bn)))
        args.append(arr)
    for arr, blk, imap in extras:
        in_specs.append(pl.BlockSpec(blk, imap))
        args.append(arr)

    def body(*refs):
        x_ref = refs[0]
        p_refs = refs[1:1 + n_p]
        w_refs = refs[1 + n_p:1 + n_p + n_w]
        e_refs = refs[1 + n_p + n_w:1 + n_p + n_w + n_e]
        o_ref = refs[1 + n_p + n_w + n_e]
        if prologue is None:
            xv = x_ref[...]
        else:
            xs_ref = refs[-1]

            @pl.when(pl.program_id(1) == 0)
            def _():
                xs_ref[...] = prologue(x_ref, *p_refs)

            xv = xs_ref[...]
        accs = []
        for (_, lead, _, transposed), w_ref in zip(ws, w_refs):
            if transposed:
                accs.append(_dot_nt(xv, w_ref[(0,) * len(lead)].astype(BF16)))
            else:
                accs.append(_dot(xv, w_ref[...].astype(BF16)))
        o_ref[...] = epilogue(accs, *e_refs).astype(o_ref.dtype)

    scratch = [] if prologue is None else [pltpu.VMEM((bm, k), BF16)]
    return pl.pallas_call(
        body, grid=(m // bm, pl.cdiv(n_out, bn)), in_specs=in_specs,
        out_specs=pl.BlockSpec((bm, bn), lambda i, j: (i, j)),
        out_shape=jax.ShapeDtypeStruct((m, n_out), out_dtype),
        scratch_shapes=scratch, compiler_params=_params(2), name=name,
    )(*args)


def _ep_plain(accs):
    return accs[0]


def _ep_swiglu(accs):
    a, b = accs
    return (a * jax.nn.sigmoid(a)) * b


def _ep_residual(accs, x_ref, g_ref):
    return x_ref[...] + g_ref[...] * accs[0]


def _ep_merge(accs, yf_ref, yc_ref, ya_ref):
    g0, g1, g2 = (jax.nn.sigmoid(a) for a in accs)
    return g0 * yf_ref[...] + g1 * yc_ref[...] + g2 * ya_ref[...]


def _rot_half_pairs(x):
    lane = lax.broadcasted_iota(jnp.int32, x.shape, 1)
    first = (lane // 16) % 2 == 0
    return jnp.where(first, -pltpu.roll(x, LANES - 16, 1), pltpu.roll(x, 16, 1))


def _ep_rope_q(accs, cos_ref, sin_ref):
    acc = accs[0]
    cos, sin = cos_ref[...], sin_ref[...]
    parts = []
    for h in range(acc.shape[1] // HEAD_PAD):
        base = h * HEAD_PAD
        parts.append(acc[:, base:base + QK_NOPE])
        pe = acc[:, base + QK_NOPE:base + HEAD_PAD]
        parts.append(pe * cos + _rot_half_pairs(pe) * sin)
    return jnp.concatenate(parts, axis=1)


def _pro_rmsnorm(x_ref, g_ref):
    return (_rms(x_ref[...]) * g_ref[...]).astype(BF16)


def _dft_tables(n, sign):
    p = jnp.arange(n, dtype=jnp.int32)
    ang = ((p[:, None] * p[None, :]) % n).astype(F32) * np.float32(2.0 * np.pi / n)
    scale = np.float32(1.0 / np.sqrt(n))
    return _split_bf16(jnp.cos(ang) * scale) + _split_bf16(jnp.sin(ang) * (sign * scale))


def _fourier_chan_body(u_ref, c_hi, c_lo, s_hi, s_lo, a_hi, a_lo, b_hi, b_lo):
    for g in range(N_FOURIER_GROUPS):
        sl = slice(g * FOURIER_GROUP, (g + 1) * FOURIER_GROUP)
        u_h, u_l = _split_bf16(u_ref[:, sl])
        ah, al = _split_bf16(_dot3(u_h, u_l, c_hi[...], c_lo[...]))
        bh, bl = _split_bf16(_dot3(u_h, u_l, s_hi[...], s_lo[...]))
        a_hi[:, sl] = ah
        a_lo[:, sl] = al
        b_hi[:, sl] = bh
        b_lo[:, sl] = bl


def _fourier_chan(z, tabs, bm=512):
    m = z.shape[0]
    tab = pl.BlockSpec((FOURIER_GROUP, FOURIER_GROUP), lambda i: (0, 0))
    row = pl.BlockSpec((bm, D_FOURIER), lambda i: (i, 0))
    return pl.pallas_call(
        _fourier_chan_body, grid=(m // bm,),
        in_specs=[row, tab, tab, tab, tab], out_specs=[row] * 4,
        out_shape=[jax.ShapeDtypeStruct((m, D_FOURIER), BF16)] * 4,
        compiler_params=_params(1), name="fourier_chan",
    )(z, *tabs)


def _fourier_seq_body(c_hi, c_lo, s_hi, s_lo, a_hi, a_lo, b_hi, b_lo, o_ref):
    y = _dot3(c_hi[...], c_lo[...], a_hi[...], a_lo[...]) + _dot3(s_hi[...], s_lo[...], b_hi[...], b_lo[...])
    o_ref[...] = y.astype(o_ref.dtype)


def _fourier_seq(ab, tabs, n_seq, seq_len, seq_blk0, bl, bn):
    nr, nc = seq_len // bl, D_FOURIER // bn
    tab = pl.BlockSpec((bl, seq_len), lambda s, c, r: (r, 0))
    dat = pl.BlockSpec((seq_len, bn), lambda s, c, r: (seq_blk0 + s, c))
    return pl.pallas_call(
        _fourier_seq_body, grid=(n_seq, nc, nr),
        in_specs=[tab] * 4 + [dat] * 4,
        out_specs=pl.BlockSpec((bl, bn), lambda s, c, r: (s * nr + r, c)),
        out_shape=jax.ShapeDtypeStruct((n_seq * seq_len, D_FOURIER), BF16),
        compiler_params=_params(3), name=f"fourier_seq{seq_len}",
    )(*tabs, *ab)


CONV_ROWS = 2048


def _conv_body(b_ref, c_ref, h_ref, w_ref, o_ref):
    i = pl.program_id(0)
    seq_len = jnp.where(i * CONV_ROWS < N_PROMPT, SEQ, DEC_SEQ)
    u = c_ref[...] * h_ref[...]
    pos = lax.broadcasted_iota(jnp.int32, u.shape, 0) & (seq_len - 1)
    prev = jnp.where(pos == 0, 0.0, pltpu.roll(u, 1, 0))
    nxt = jnp.where(pos == seq_len - 1, 0.0, pltpu.roll(u, CONV_ROWS - 1, 0))
    w = w_ref[...]
    o_ref[...] = (b_ref[...] * (w[0:1] * prev + w[1:2] * u + w[2:3] * nxt)).astype(o_ref.dtype)


def _conv(z, conv_w, layer, bc=256):
    m = z.shape[0]
    nb = D_CONV // bc
    col0 = D_FOURIER // bc

    def part(k):
        return pl.BlockSpec((CONV_ROWS, bc), lambda i, j: (i, col0 + k * nb + j))

    return pl.pallas_call(
        _conv_body, grid=(m // CONV_ROWS, nb),
        in_specs=[part(0), part(1), part(2), pl.BlockSpec((None, 3, bc), lambda i, j: (layer, 0, j))],
        out_specs=pl.BlockSpec((CONV_ROWS, bc), lambda i, j: (i, j)),
        out_shape=jax.ShapeDtypeStruct((m, D_CONV), BF16),
        compiler_params=_params(2), name="short_conv",
    )(z, z, z, conv_w)


def _rope_k_body(x_ref, cos_ref, sin_ref, o_ref):
    x = x_ref[...]
    o_ref[...] = (x * cos_ref[...] + _rot_half_pairs(x) * sin_ref[...]).astype(o_ref.dtype)


def _rope_k(x, cos, sin, bm=512):
    m = x.shape[0]
    blk = pl.BlockSpec((bm, LANES), lambda i: (i, 0))
    return pl.pallas_call(
        _rope_k_body, grid=(m // bm,), in_specs=[blk, blk, blk], out_specs=blk,
        out_shape=jax.ShapeDtypeStruct((m, LANES), BF16),
        compiler_params=_params(1), name="rope_k",
    )(x, cos, sin)


def _attn_body(q_ref, kv_ref, kp_ref, o_ref, kcat_ref, *, heads):
    @pl.when(pl.program_id(2) == 0)
    def _():
        kp = kp_ref[...]
        for h in range(heads):
            kcat_ref[h, :, :QK_NOPE] = kv_ref[:, h * HEAD_PAD:h * HEAD_PAD + QK_NOPE]
            kcat_ref[h, :, QK_NOPE:] = kp

    for h in range(heads):
        q = q_ref[:, h * HEAD_PAD:(h + 1) * HEAD_PAD]
        v = kv_ref[:, h * HEAD_PAD + QK_NOPE:(h + 1) * HEAD_PAD]
        s = _dot_nt(q, kcat_ref[h]) * np.float32((QK_NOPE + QK_ROPE) ** -0.5)
        p = jnp.exp(s - jnp.max(s, axis=-1, keepdims=True))
        l = jnp.sum(p, axis=-1, keepdims=True)
        o_ref[:, h * V_HEAD:(h + 1) * V_HEAD] = (_dot(p.astype(BF16), v) / l).astype(o_ref.dtype)


def _attention(q, kv, kpe, n_seq, lq, lk, q_row0, k_row0, bq, heads):
    nq = lq // bq
    return pl.pallas_call(
        functools.partial(_attn_body, heads=heads), grid=(n_seq, N_HEADS // heads, nq),
        in_specs=[
            pl.BlockSpec((bq, heads * HEAD_PAD), lambda b, h, i: (q_row0 // bq + b * nq + i, h)),
            pl.BlockSpec((lk, heads * (QK_NOPE + V_HEAD)), lambda b, h, i: (k_row0 // lk + b, h)),
            pl.BlockSpec((lk, LANES), lambda b, h, i: (k_row0 // lk + b, 0)),
        ],
        out_specs=pl.BlockSpec((bq, heads * V_HEAD), lambda b, h, i: (b * nq + i, h)),
        out_shape=jax.ShapeDtypeStruct((n_seq * lq, N_HEADS * V_HEAD), BF16),
        scratch_shapes=[pltpu.VMEM((heads, lk, HEAD_PAD), BF16)],
        compiler_params=_params(3), name=f"attention{lq}",
    )(q, kv, kpe)


def _route(idx):
    flat_e = idx.reshape(-1)
    n = flat_e.shape[0]
    onehot = (flat_e[:, None] == jnp.arange(N_EXPERTS, dtype=jnp.int32)[None, :]).astype(jnp.int32)
    csum = jnp.cumsum(onehot, axis=0)
    rank = jnp.take_along_axis(csum, flat_e[:, None], axis=1)[:, 0] - 1
    counts = csum[-1]
    n_tile = (counts + MOE_BM - 1) // MOE_BM
    tile_end = jnp.cumsum(n_tile)
    tile_start = tile_end - n_tile
    pos = tile_start[flat_e] * MOE_BM + rank
    n_valid = tile_end[-1]
    t = jnp.arange(MOE_TILES, dtype=jnp.int32)
    tc = jnp.minimum(t, n_valid - 1)
    tile_e = jnp.sum((tile_end[None, :] <= tc[:, None]).astype(jnp.int32), axis=1)
    rows = jnp.clip(counts[tile_e] - (tc - tile_start[tile_e]) * MOE_BM, 0, MOE_BM)
    tile_rows = jnp.where(t < n_valid, rows, 0)
    src = jnp.zeros((MOE_TILES * MOE_BM,), jnp.int32).at[pos].set(jnp.arange(n, dtype=jnp.int32) // TOP_K)
    i32 = jnp.int32
    return pos.astype(i32), src, tile_e.astype(i32), tile_rows.astype(i32), n_valid.reshape(1).astype(i32)


GATHER_UNROLL = 8


def _gather_body(src_ref, nr_ref, h_hbm, o_ref, buf, sem):
    t, s = pl.program_id(0), pl.program_id(1)
    base = t * MOE_BM + s * GATHER_ROWS
    active = s * GATHER_ROWS < nr_ref[t]

    def row_copy(r, src_row):
        return pltpu.make_async_copy(h_hbm.at[pl.ds(src_row, 1)], buf.at[pl.ds(r, 1)], sem)

    @pl.when(active)
    def _():
        def start(g, c):
            for u in range(GATHER_UNROLL):
                r = g * GATHER_UNROLL + u
                row_copy(r, src_ref[base + r]).start()
            return c

        def wait(g, c):
            for u in range(GATHER_UNROLL):
                row_copy(g * GATHER_UNROLL + u, 0).wait()
            return c

        lax.fori_loop(0, GATHER_ROWS // GATHER_UNROLL, start, 0)
        lax.fori_loop(0, GATHER_ROWS // GATHER_UNROLL, wait, 0)
        o_ref[...] = buf[...].astype(o_ref.dtype)

    @pl.when(jnp.logical_not(active))
    def _():
        o_ref[...] = jnp.zeros_like(o_ref)


def _gather_rows(h, src, tile_rows):
    n_sub = MOE_BM // GATHER_ROWS
    grid_spec = pltpu.PrefetchScalarGridSpec(
        num_scalar_prefetch=2, grid=(MOE_TILES, n_sub),
        in_specs=[pl.BlockSpec(memory_space=pl.ANY)],
        out_specs=pl.BlockSpec((GATHER_ROWS, D_MODEL), lambda t, s, src, nr: (t * n_sub + s, 0)),
        scratch_shapes=[pltpu.VMEM((GATHER_ROWS, D_MODEL), F32), pltpu.SemaphoreType.DMA(())],
    )
    return pl.pallas_call(
        _gather_body, grid_spec=grid_spec,
        out_shape=jax.ShapeDtypeStruct((MOE_TILES * MOE_BM, D_MODEL), BF16),
        compiler_params=_params(2), name="moe_gather",
    )(src, tile_rows, h)


def _sub_blocks(nr_ref):
    return (nr_ref[pl.program_id(0)] + MOE_SUB - 1) // MOE_SUB


def _moe_up_body(te_ref, nr_ref, nv_ref, x_ref, w1_ref, w3_ref, o_ref):
    n_sub = _sub_blocks(nr_ref)
    for n in range(MOE_BM // MOE_SUB + 1):
        @pl.when(n_sub == n)
        def _(n=n):
            m = n * MOE_SUB
            if m:
                xv = x_ref[:m, :]
                a = _dot(xv, w1_ref[...].astype(BF16))
                b = _dot(xv, w3_ref[...].astype(BF16))
                o_ref[:m, :] = ((a * jax.nn.sigmoid(a)) * b).astype(o_ref.dtype)
            if m < MOE_BM:
                o_ref[m:, :] = jnp.zeros((MOE_BM - m, o_ref.shape[1]), o_ref.dtype)


def _moe_up(xs, w1, w3, moe_layer, tile_e, tile_rows, n_valid, bn=256):
    gj = D_FF_EXPERT // bn

    def row(t, nv):
        return jnp.minimum(t, nv[0] - 1)

    def col(t, j, nv):
        return jnp.where(t < nv[0], j, gj - 1)

    w_spec = pl.BlockSpec((None, None, D_MODEL, bn),
                          lambda t, j, te, nr, nv: (moe_layer, te[t], 0, col(t, j, nv)))
    grid_spec = pltpu.PrefetchScalarGridSpec(
        num_scalar_prefetch=3, grid=(MOE_TILES, gj),
        in_specs=[pl.BlockSpec((MOE_BM, D_MODEL), lambda t, j, te, nr, nv: (row(t, nv), 0)), w_spec, w_spec],
        out_specs=pl.BlockSpec((MOE_BM, bn), lambda t, j, te, nr, nv: (t, j)),
    )
    return pl.pallas_call(
        _moe_up_body, grid_spec=grid_spec,
        out_shape=jax.ShapeDtypeStruct((xs.shape[0], D_FF_EXPERT), BF16),
        compiler_params=_params(2), name="moe_up",
    )(tile_e, tile_rows, n_valid, xs, w1, w3)


def _moe_down_body(te_ref, nr_ref, nv_ref, x_ref, w_ref, o_ref):
    n_sub = _sub_blocks(nr_ref)
    first = pl.program_id(2) == 0
    for n in range(MOE_BM // MOE_SUB + 1):
        m = n * MOE_SUB

        @pl.when((n_sub == n) & first)
        def _(m=m):
            if m:
                o_ref[:m, :] = _dot(x_ref[:m, :], w_ref[...].astype(BF16))
            if m < MOE_BM:
                o_ref[m:, :] = jnp.zeros((MOE_BM - m, o_ref.shape[1]), o_ref.dtype)

        if m:
            @pl.when((n_sub == n) & jnp.logical_not(first))
            def _(m=m):
                o_ref[:m, :] += _dot(x_ref[:m, :], w_ref[...].astype(BF16))


def _moe_down(mid, w2, moe_layer, tile_e, tile_rows, n_valid, bn=1024, bk=2048):
    gj, gk = D_MODEL // bn, D_FF_EXPERT // bk

    def row(t, nv):
        return jnp.minimum(t, nv[0] - 1)

    def clamp(t, v, last, nv):
        return jnp.where(t < nv[0], v, last)

    grid_spec = pltpu.PrefetchScalarGridSpec(
        num_scalar_prefetch=3, grid=(MOE_TILES, gj, gk),
        in_specs=[
            pl.BlockSpec((MOE_BM, bk), lambda t, j, k, te, nr, nv: (row(t, nv), clamp(t, k, gk - 1, nv))),
            pl.BlockSpec((None, None, bk, bn),
                         lambda t, j, k, te, nr, nv: (moe_layer, te[t], clamp(t, k, gk - 1, nv),
                                                      clamp(t, j, gj - 1, nv))),
        ],
        out_specs=pl.BlockSpec((MOE_BM, bn), lambda t, j, k, te, nr, nv: (t, j)),
    )
    return pl.pallas_call(
        _moe_down_body, grid_spec=grid_spec,
        out_shape=jax.ShapeDtypeStruct((mid.shape[0], D_MODEL), F32),
        compiler_params=_params(3), name="moe_down",
    )(tile_e, tile_rows, n_valid, mid, w2)


def _combine_body(pos_ref, y_hbm, wt_ref, x_ref, g_ref, o_ref, buf, sem):
    base = pl.program_id(0) * COMBINE_ROWS

    def row_copy(r, c, src_row):
        return pltpu.make_async_copy(y_hbm.at[pl.ds(src_row, 1)], buf.at[c, pl.ds(r, 1)], sem)

    def start(g, carry):
        for u in range(GATHER_UNROLL):
            r = g * GATHER_UNROLL + u
            for c in range(TOP_K):
                row_copy(r, c, pos_ref[(base + r) * TOP_K + c]).start()
        return carry

    def wait(g, carry):
        for u in range(GATHER_UNROLL):
            for c in range(TOP_K):
                row_copy(g * GATHER_UNROLL + u, c, 0).wait()
        return carry

    lax.fori_loop(0, COMBINE_ROWS // GATHER_UNROLL, start, 0)
    lax.fori_loop(0, COMBINE_ROWS // GATHER_UNROLL, wait, 0)
    g = g_ref[...]
    for r0 in range(0, COMBINE_ROWS, LANES):
        rows = pl.ds(r0, LANES)
        wt = wt_ref[rows, :]
        f = wt[:, 0:1] * buf[0, rows, :] + wt[:, 1:2] * buf[1, rows, :]
        o_ref[rows, :] = x_ref[rows, :] + g * f


def _moe_combine(y, pos, wts, x, mods, k_gate):
    m = x.shape[0]
    row = pl.BlockSpec((COMBINE_ROWS, D_MODEL), lambda i, p: (i, 0))
    grid_spec = pltpu.PrefetchScalarGridSpec(
        num_scalar_prefetch=1, grid=(m // COMBINE_ROWS,),
        in_specs=[
            pl.BlockSpec(memory_space=pl.ANY),
            pl.BlockSpec((COMBINE_ROWS, LANES), lambda i, p: (i, 0)),
            row,
            pl.BlockSpec((None, 1, D_MODEL), lambda i, p: (_cond_of_tile(i, COMBINE_ROWS) * 6 + k_gate, 0, 0)),
        ],
        out_specs=row,
        scratch_shapes=[pltpu.VMEM((TOP_K, COMBINE_ROWS, D_MODEL), F32), pltpu.SemaphoreType.DMA(())],
    )
    return pl.pallas_call(
        _combine_body, grid_spec=grid_spec,
        out_shape=jax.ShapeDtypeStruct((m, D_MODEL), F32),
        compiler_params=_params(1), name="moe_combine",
    )(pos, y, wts, x, mods)


def _rope_tables():
    n_rows = DEC_SEQ // GRID_W
    row = jnp.repeat(jnp.arange(n_rows), GRID_W).astype(F32)
    col = jnp.tile(jnp.arange(GRID_W), n_rows).astype(F32)
    half = QK_ROPE // 2
    inv = ROPE_BASE ** (-(jnp.arange(half // 2, dtype=F32) * 2.0) / half)
    ang_r = row[:, None] * inv
    ang_c = col[:, None] * inv
    ang = jnp.concatenate([ang_r, ang_r, ang_c, ang_c], axis=-1)
    pad1 = jnp.ones((DEC_SEQ, LANES - QK_ROPE), F32)
    cos = jnp.concatenate([jnp.cos(ang), pad1], axis=1)
    sin = jnp.concatenate([jnp.sin(ang), 0 * pad1], axis=1)
    return cos, sin


def _mod_spec(bm, bn, which):
    return ((None, 1, bn), lambda i, j: (_cond_of_tile(i, bm) * 6 + which, 0, j))


def kernel(x_prompt, x_sample, cache_ckv, cache_kpe, c, c_ctx, norm1_g, norm2_g, w_ada, b_ada, w_in,
           w_fourier_out, conv_w, w_conv_out, q_norm_g, w_uq, kv_norm_g, w_ukv, w_mla_out, w_o,
           ffn_w1, ffn_w3, ffn_w2, router_w, router_b, moe_w1, moe_w3, moe_w2, final_g):
    bm = 1024
    x = jnp.concatenate([x_prompt.reshape(N_PROMPT, D_MODEL), x_sample.reshape(N_SAMPLE, D_MODEL)], axis=0)
    w_in_t = jnp.swapaxes(w_in, 1, 2)

    cond8 = jnp.concatenate([c_ctx[None, :], c, jnp.zeros((8 - 1 - DEC_BATCH, D_MODEL), F32)], axis=0)
    mods_all = _ada(cond8, w_ada, b_ada)[:, :1 + DEC_BATCH].reshape(DEPTH, (1 + DEC_BATCH) * 6, 1, D_MODEL)

    cos, sin = _rope_tables()
    one = jnp.ones((N_PROMPT, LANES), F32)
    cos_q = jnp.concatenate([one] + [cos] * DEC_BATCH, axis=0)
    sin_q = jnp.concatenate([0 * one] + [sin] * DEC_BATCH, axis=0)
    one_p = jnp.ones((PAST_LEN, LANES), F32)
    cos_k = jnp.concatenate([one_p, cos] * DEC_BATCH + [one], axis=0)
    sin_k = jnp.concatenate([0 * one_p, sin] * DEC_BATCH + [0 * one], axis=0)
    tab_chan = _dft_tables(FOURIER_GROUP, 1.0)
    tab_prompt = _dft_tables(SEQ, -1.0)
    tab_sample = _dft_tables(DEC_SEQ, -1.0)

    ckv_out, kpe_out = [], []
    for layer in range(DEPTH):
        mods = mods_all[layer]
        uq = w_uq[layer].reshape(Q_LORA, N_HEADS, QK_NOPE + QK_ROPE)
        uq = jnp.concatenate([uq, jnp.zeros((Q_LORA, N_HEADS, HEAD_PAD - QK_NOPE - QK_ROPE), F32)], axis=-1)
        uq = uq.reshape(Q_LORA, N_HEADS * HEAD_PAD).astype(BF16)

        h = _norm_mod(x, norm1_g, layer, mods, 1, 0)
        z = _mm("in_proj", h, [(w_in_t, (layer,), 0, True)], _ep_plain, bm=bm, bn=512, n_out=Z_MAIN, out_dtype=F32)

        ab = _fourier_chan(z, tab_chan)
        yf_pre = jnp.concatenate([
            _fourier_seq(ab, tab_prompt, BATCH, SEQ, 0, SEQ, D_FOURIER),
            _fourier_seq(ab, tab_sample, DEC_BATCH, DEC_SEQ, N_PROMPT // DEC_SEQ, 512, 512)], axis=0)
        y_f = _mm("fourier_out", yf_pre, [(w_fourier_out, (layer,), 0, False)], _ep_plain,
                  bm=bm, bn=2048, n_out=D_MODEL, out_dtype=F32)

        yc_pre = _conv(z, conv_w, layer)
        y_c = _mm("conv_out", yc_pre, [(w_conv_out, (layer,), 0, False)], _ep_plain,
                  bm=bm, bn=2048, n_out=D_MODEL, out_dtype=F32)

        q_col = (D_FOURIER + 3 * D_CONV) // Q_LORA
        q = _mm("q_proj", z, [(uq, (), 0, False)], _ep_rope_q, bm=bm, bn=1024, n_out=N_HEADS * HEAD_PAD,
                out_dtype=BF16, k=Q_LORA, x_kblk=q_col, prologue=_pro_rmsnorm,
                pro_extras=[(q_norm_g.reshape(DEPTH, 1, Q_LORA), (None, 1, Q_LORA), lambda i, j: (layer, 0, 0))],
                extras=[(cos_q, (bm, LANES), lambda i, j: (i, 0)), (sin_q, (bm, LANES), lambda i, j: (i, 0))])
        kv_col = (D_FOURIER + 3 * D_CONV + Q_LORA) // KV_LORA
        c_kv = _rmsnorm(z, kv_norm_g.reshape(DEPTH, 1, KV_LORA), layer, KV_LORA, kv_col)
        k_pe = z[:, GATE_COL0 - QK_ROPE:GATE_COL0]
        ckv_out.append(c_kv[:N_PROMPT].reshape(BATCH, SEQ, KV_LORA))
        kpe_out.append(k_pe[:N_PROMPT].reshape(BATCH, SEQ, QK_ROPE))
        ckv_s = jnp.concatenate([cache_ckv[:, layer], c_kv[N_PROMPT:].reshape(DEC_BATCH, DEC_SEQ, KV_LORA)], axis=1)
        ckv_ext = jnp.concatenate([ckv_s.reshape(-1, KV_LORA), c_kv[:N_PROMPT]], axis=0).astype(BF16)
        kpe_s = jnp.concatenate([cache_kpe[:, layer], k_pe[N_PROMPT:].reshape(DEC_BATCH, DEC_SEQ, QK_ROPE)], axis=1)
        kpe_ext = jnp.concatenate([kpe_s.reshape(-1, QK_ROPE), k_pe[:N_PROMPT]], axis=0)
        kpe_ext = jnp.concatenate([kpe_ext, jnp.zeros((N_KV_ROWS, LANES - QK_ROPE), F32)], axis=1)
        kpe_ext = _rope_k(kpe_ext, cos_k, sin_k)
        kv = _mm("kv_proj", ckv_ext, [(w_ukv, (layer,), 0, False)], _ep_plain,
                 bm=512, bn=2048, n_out=N_HEADS * (QK_NOPE + V_HEAD), out_dtype=BF16)
        attn = jnp.concatenate([
            _attention(q, kv, kpe_ext, BATCH, SEQ, SEQ, 0, DEC_BATCH * KV_LEN, SEQ, N_HEADS),
            _attention(q, kv, kpe_ext, DEC_BATCH, DEC_SEQ, KV_LEN, N_PROMPT, 0, 512, 8)], axis=0)
        y_a = _mm("mla_out", attn, [(w_mla_out, (layer,), 0, False)], _ep_plain,
                  bm=bm, bn=1024, n_out=D_MODEL, out_dtype=F32)

        tile = ((bm, 256), lambda i, j: (i, j))
        merged = _mm("gate_merge", h,
                     [(w_in_t, (layer,), GATE_COL0 + b * D_MODEL, True) for b in range(N_BRANCH)], _ep_merge,
                     bm=bm, bn=256, n_out=D_MODEL, out_dtype=BF16, x_buffers=1,
                     extras=[(y_f,) + tile, (y_c,) + tile, (y_a,) + tile])
        x = _mm("out_proj", merged, [(w_o, (layer,), 0, False)], _ep_residual, bm=bm, bn=512, n_out=D_MODEL,
                out_dtype=F32, extras=[(x, (bm, 512), lambda i, j: (i, j)), (mods,) + _mod_spec(bm, 512, 2)])

        j = layer // 2
        if layer % 2 == 0:
            h2 = _norm_mod(x, norm2_g, layer, mods, 4, 3)
            mid = _mm("ffn_up", h2, [(ffn_w1, (j,), 0, False), (ffn_w3, (j,), 0, False)], _ep_swiglu,
                      bm=bm, bn=256, n_out=D_FF, out_dtype=BF16)
            x = _mm("ffn_down", mid, [(ffn_w2[j].astype(BF16), (), 0, False)], _ep_residual, bm=512, bn=256,
                    n_out=D_MODEL, out_dtype=F32,
                    extras=[(x, (512, 256), lambda i, j: (i, j)), (mods,) + _mod_spec(512, 256, 5)])
        else:
            rw = jnp.concatenate([router_w[j], jnp.zeros((D_MODEL, LANES - N_EXPERTS), F32)], axis=1)
            rb = jnp.concatenate([router_b[j], jnp.zeros((LANES - N_EXPERTS,), F32)])[None, :]
            h2f, wts, idx = _norm_mod(x, norm2_g, layer, mods, 4, 3, router=_split_bf16(rw) + (rb,))
            pos, src, tile_e, tile_rows, n_valid = _route(idx[:, :TOP_K])
            xs = _gather_rows(h2f, src, tile_rows)
            mid = _moe_up(xs, moe_w1, moe_w3, j, tile_e, tile_rows, n_valid)
            y = _moe_down(mid, moe_w2, j, tile_e, tile_rows, n_valid)
            x = _moe_combine(y, pos, wts, x, mods, 5)

    fg = final_g.reshape(1, 1, D_MODEL)
    y_prompt = _rmsnorm(x, fg, 0, D_MODEL, 0, bm=256, row0=0, rows=N_PROMPT)
    y_sample = _rmsnorm(x, fg, 0, D_MODEL, 0, bm=256, row0=N_PROMPT, rows=N_SAMPLE)
    return (y_prompt.reshape(BATCH, SEQ, D_MODEL),
            y_sample.reshape(DEC_BATCH, DEC_SEQ, D_MODEL),
            jnp.stack(ckv_out, axis=1),
            jnp.stack(kpe_out, axis=1))
```

```python
import functools

import jax
import jax.numpy as jnp
import numpy as np
from jax import lax
from jax.experimental import pallas as pl
from jax.experimental.pallas import tpu as pltpu

F32 = jnp.float32
BF16 = jnp.bfloat16

D_MODEL = 4096
BATCH = 16
SEQ = 256
DEPTH = 2
DEC_BATCH = 2
DEC_SEQ = 2048
PAST_LEN = 256
GRID_W = 64
N_FOURIER_GROUPS = 4
FOURIER_GROUP = 256
D_FOURIER = N_FOURIER_GROUPS * FOURIER_GROUP
D_CONV = 1024
N_HEADS = 16
Q_LORA = 1024
KV_LORA = 512
QK_NOPE = 128
QK_ROPE = 64
V_HEAD = 128
ROPE_BASE = 10000.0
N_BRANCH = 3
D_FF = 11008
N_EXPERTS = 8
TOP_K = 2
D_FF_EXPERT = 14336
EPS = 1e-6

N_PROMPT = BATCH * SEQ
N_SAMPLE = DEC_BATCH * DEC_SEQ
N_TOK = N_PROMPT + N_SAMPLE
KV_LEN = PAST_LEN + DEC_SEQ
N_KV_ROWS = DEC_BATCH * KV_LEN + N_PROMPT
GATE_COL0 = D_FOURIER + 3 * D_CONV + Q_LORA + KV_LORA + QK_ROPE
Z_MAIN = 6144
HEAD_PAD = 256
LANES = 128

VMEM_LIMIT = 56 * 1024 * 1024
MOE_SUB = 256
MOE_BM = 5 * MOE_SUB
MOE_TILES = (TOP_K * N_TOK) // MOE_BM + N_EXPERTS
GATHER_ROWS = 256
COMBINE_ROWS = 512


def _params(n_axes, vmem=VMEM_LIMIT):
    return pltpu.CompilerParams(dimension_semantics=("arbitrary",) * n_axes, vmem_limit_bytes=vmem)


def _cond_of_tile(i, bm):
    r0 = i * bm
    return jnp.where(r0 < N_PROMPT, 0, 1 + (r0 - N_PROMPT) // DEC_SEQ)


def _split_bf16(x):
    hi = x.astype(BF16)
    lo = (x - hi.astype(F32)).astype(BF16)
    return hi, lo


def _dot(a, b):
    return jnp.dot(a, b, preferred_element_type=F32)


def _dot3(a_hi, a_lo, b_hi, b_lo):
    return _dot(a_hi, b_hi) + _dot(a_hi, b_lo) + _dot(a_lo, b_hi)


def _ada_body(c_ref, w_ref, b_ref, o_ref):
    c = c_ref[...]
    s = (c * jax.nn.sigmoid(c)).astype(BF16)
    o_ref[...] = _dot(s, w_ref[...].astype(BF16)) + b_ref[...]


def _ada(cond8, w_ada, b_ada, bn=512):
    n = w_ada.shape[-1]
    return pl.pallas_call(
        _ada_body,
        grid=(DEPTH, n // bn),
        in_specs=[
            pl.BlockSpec((8, D_MODEL), lambda l, j: (0, 0)),
            pl.BlockSpec((None, D_MODEL, bn), lambda l, j: (l, 0, j)),
            pl.BlockSpec((None, 1, bn), lambda l, j: (l, 0, j)),
        ],
        out_specs=pl.BlockSpec((None, 8, bn), lambda l, j: (l, 0, j)),
        out_shape=jax.ShapeDtypeStruct((DEPTH, 8, n), F32),
        compiler_params=_params(2),
        name="ada",
    )(cond8, w_ada, b_ada.reshape(DEPTH, 1, n))


def _rms(x):
    return x * lax.rsqrt(jnp.mean(x * x, axis=-1, keepdims=True) + EPS)


def _norm_mod_body(x_ref, g_ref, sc_ref, sh_ref, o_ref):
    y = _rms(x_ref[...]) * g_ref[...]
    o_ref[...] = (y * (1 + sc_ref[...]) + sh_ref[...]).astype(o_ref.dtype)


def _norm_mod_router_body(x_ref, g_ref, sc_ref, sh_ref, rw_hi_ref, rw_lo_ref, rb_ref,
                          of_ref, wt_ref, ix_ref):
    y = _rms(x_ref[...]) * g_ref[...]
    h = y * (1 + sc_ref[...]) + sh_ref[...]
    of_ref[...] = h
    h_hi, h_lo = _split_bf16(h)
    logits = _dot3(h_hi, h_lo, rw_hi_ref[...], rw_lo_ref[...]) + rb_ref[...]
    lane = lax.broadcasted_iota(jnp.int32, logits.shape, 1)
    lanef = lane.astype(F32)
    neg = jnp.float32(-jnp.inf)
    lg = jnp.where(lane < N_EXPERTS, logits, neg)
    m1 = jnp.max(lg, axis=-1, keepdims=True)
    i1 = jnp.min(jnp.where(lg == m1, lanef, float(LANES)), axis=-1, keepdims=True)
    lg2 = jnp.where(lanef == i1, neg, lg)
    m2 = jnp.max(lg2, axis=-1, keepdims=True)
    i2 = jnp.min(jnp.where(lg2 == m2, lanef, float(LANES)), axis=-1, keepdims=True)
    e = jnp.exp(m2 - m1)
    w1 = 1.0 / (1.0 + e)
    w2 = e / (1.0 + e)
    wt_ref[...] = jnp.where(lane == 0, w1, jnp.where(lane == 1, w2, 0.0))
    ix_ref[...] = jnp.where(lane == 0, i1, jnp.where(lane == 1, i2, 0.0)).astype(jnp.int32)


def _norm_mod(x, gain, layer, mods, k_scale, k_shift, router=None, bm=256):
    m = x.shape[0]
    in_specs = [
        pl.BlockSpec((bm, D_MODEL), lambda i: (i, 0)),
        pl.BlockSpec((None, 1, D_MODEL), lambda i: (layer, 0, 0)),
        pl.BlockSpec((None, 1, D_MODEL), lambda i: (_cond_of_tile(i, bm) * 6 + k_scale, 0, 0)),
        pl.BlockSpec((None, 1, D_MODEL), lambda i: (_cond_of_tile(i, bm) * 6 + k_shift, 0, 0)),
    ]
    args = [x, gain.reshape(DEPTH, 1, D_MODEL), mods, mods]
    row = pl.BlockSpec((bm, D_MODEL), lambda i: (i, 0))
    if router is None:
        return pl.pallas_call(
            _norm_mod_body, grid=(m // bm,), in_specs=in_specs, out_specs=row,
            out_shape=jax.ShapeDtypeStruct((m, D_MODEL), BF16),
            compiler_params=_params(1), name="norm_mod",
        )(*args)
    rw_hi, rw_lo, rb = router
    small = pl.BlockSpec((bm, LANES), lambda i: (i, 0))
    in_specs += [
        pl.BlockSpec((D_MODEL, LANES), lambda i: (0, 0)),
        pl.BlockSpec((D_MODEL, LANES), lambda i: (0, 0)),
        pl.BlockSpec((1, LANES), lambda i: (0, 0)),
    ]
    return pl.pallas_call(
        _norm_mod_router_body, grid=(m // bm,), in_specs=in_specs,
        out_specs=[row, small, small],
        out_shape=[jax.ShapeDtypeStruct((m, D_MODEL), F32),
                   jax.ShapeDtypeStruct((m, LANES), F32), jax.ShapeDtypeStruct((m, LANES), jnp.int32)],
        compiler_params=_params(1), name="norm_mod_router",
    )(*args, rw_hi, rw_lo, rb)


def _rmsnorm_body(x_ref, g_ref, o_ref):
    o_ref[...] = (_rms(x_ref[...]) * g_ref[...]).astype(o_ref.dtype)


def _rmsnorm(x, gain3, lead, width, col_blk, bm=512, row0=0, rows=None):
    m = rows or x.shape[0]
    return pl.pallas_call(
        _rmsnorm_body, grid=(m // bm,),
        in_specs=[pl.BlockSpec((bm, width), lambda i: (row0 // bm + i, col_blk)),
                  pl.BlockSpec((None, 1, width), lambda i: (lead, 0, 0))],
        out_specs=pl.BlockSpec((bm, width), lambda i: (i, 0)),
        out_shape=jax.ShapeDtypeStruct((m, width), F32),
        compiler_params=_params(1), name="rmsnorm",
    )(x, gain3)


def _dot_nt(a, b):
    return lax.dot_general(a, b, (((1,), (1,)), ((), ())), preferred_element_type=F32)


def _mm(name, x, ws, epilogue, *, bm, bn, n_out, out_dtype, k=None, x_kblk=0, extras=(),
        prologue=None, pro_extras=(), x_buffers=2):
    m = x.shape[0]
    k = k or x.shape[1]
    n_w, n_e, n_p = len(ws), len(extras), len(pro_extras)
    x_mode = {} if x_buffers == 2 else {"pipeline_mode": pl.Buffered(x_buffers)}
    in_specs = [pl.BlockSpec((bm, k), lambda i, j: (i, x_kblk), **x_mode)]
    args = [x]
    for arr, blk, imap in pro_extras:
        in_specs.append(pl.BlockSpec(blk, imap))
        args.append(arr)
    for arr, lead, col0, transposed in ws:
        squeeze = (None,) * len(lead)
        if transposed:
            in_specs.append(pl.BlockSpec((pl.Element(1),) * len(lead) + (pl.Element(bn), pl.Element(k)),
                                         lambda i, j, lead=lead, col0=col0:
                                         lead + (pl.multiple_of(col0 + j * bn, 8), 0)))
        else:
            assert col0 % bn == 0
            in_specs.append(pl.BlockSpec(squeeze + (k, bn),
                                         lambda i, j, lead=lead, col0=col0: lead + (0, j + col0 // bn)))
        args.append(arr)
    for arr, blk, imap in extras:
        in_specs.append(pl.BlockSpec(blk, imap))
        args.append(arr)

    def body(*refs):
        x_ref = refs[0]
        p_refs = refs[1:1 + n_p]
        w_refs = refs[1 + n_p:1 + n_p + n_w]
        e_refs = refs[1 + n_p + n_w:1 + n_p + n_w + n_e]
        o_ref = refs[1 + n_p + n_w + n_e]
        if prologue is None:
            xv = x_ref[...]
        else:
            xs_ref = refs[-1]

            @pl.when(pl.program_id(1) == 0)
            def _():
                xs_ref[...] = prologue(x_ref, *p_refs)

            xv = xs_ref[...]
        accs = []
        for (_, lead, _, transposed), w_ref in zip(ws, w_refs):
            if transposed:
                accs.append(_dot_nt(xv, w_ref[(0,) * len(lead)].astype(BF16)))
            else:
                accs.append(_dot(xv, w_ref[...].astype(BF16)))
        o_ref[...] = epilogue(accs, *e_refs).astype(o_ref.dtype)

    scratch = [] if prologue is None else [pltpu.VMEM((bm, k), BF16)]
    return pl.pallas_call(
        body, grid=(m // bm, pl.cdiv(n_out, bn)), in_specs=in_specs,
        out_specs=pl.BlockSpec((bm, bn), lambda i, j: (i, j)),
        out_shape=jax.ShapeDtypeStruct((m, n_out), out_dtype),
        scratch_shapes=scratch, compiler_params=_params(2), name=name,
    )(*args)


def _ep_plain(accs):
    return accs[0]


def _ep_swiglu(accs):
    a, b = accs
    return (a * jax.nn.sigmoid(a)) * b


def _ep_residual(accs, x_ref, g_ref):
    return x_ref[...] + g_ref[...] * accs[0]


def _ep_merge(accs, yf_ref, yc_ref, ya_ref):
    g0, g1, g2 = (jax.nn.sigmoid(a) for a in accs)
    return g0 * yf_ref[...] + g1 * yc_ref[...] + g2 * ya_ref[...]


def _rot_half_pairs(x):
    lane = lax.broadcasted_iota(jnp.int32, x.shape, 1)
    first = (lane // 16) % 2 == 0
    return jnp.where(first, -pltpu.roll(x, LANES - 16, 1), pltpu.roll(x, 16, 1))


def _ep_rope_q(accs, cos_ref, sin_ref):
    acc = accs[0]
    cos, sin = cos_ref[...], sin_ref[...]
    parts = []
    for h in range(acc.shape[1] // HEAD_PAD):
        base = h * HEAD_PAD
        parts.append(acc[:, base:base + QK_NOPE])
        pe = acc[:, base + QK_NOPE:base + HEAD_PAD]
        parts.append(pe * cos + _rot_half_pairs(pe) * sin)
    return jnp.concatenate(parts, axis=1)


def _pro_rmsnorm(x_ref, g_ref):
    return (_rms(x_ref[...]) * g_ref[...]).astype(BF16)


def _dft_tables(n, sign):
    p = jnp.arange(n, dtype=jnp.int32)
    ang = ((p[:, None] * p[None, :]) % n).astype(F32) * np.float32(2.0 * np.pi / n)
    scale = np.float32(1.0 / np.sqrt(n))
    return _split_bf16(jnp.cos(ang) * scale) + _split_bf16(jnp.sin(ang) * (sign * scale))


def _fourier_chan_body(u_ref, c_hi, c_lo, s_hi, s_lo, a_hi, a_lo, b_hi, b_lo):
    for g in range(N_FOURIER_GROUPS):
        sl = slice(g * FOURIER_GROUP, (g + 1) * FOURIER_GROUP)
        u_h, u_l = _split_bf16(u_ref[:, sl])
        ah, al = _split_bf16(_dot3(u_h, u_l, c_hi[...], c_lo[...]))
        bh, bl = _split_bf16(_dot3(u_h, u_l, s_hi[...], s_lo[...]))
        a_hi[:, sl] = ah
        a_lo[:, sl] = al
        b_hi[:, sl] = bh
        b_lo[:, sl] = bl


def _fourier_chan(z, tabs, bm=512):
    m = z.shape[0]
    tab = pl.BlockSpec((FOURIER_GROUP, FOURIER_GROUP), lambda i: (0, 0))
    row = pl.BlockSpec((bm, D_FOURIER), lambda i: (i, 0))
    return pl.pallas_call(
        _fourier_chan_body, grid=(m // bm,),
        in_specs=[row, tab, tab, tab, tab], out_specs=[row] * 4,
        out_shape=[jax.ShapeDtypeStruct((m, D_FOURIER), BF16)] * 4,
        compiler_params=_params(1), name="fourier_chan",
    )(z, *tabs)


def _fourier_seq_body(c_hi, c_lo, s_hi, s_lo, a_hi, a_lo, b_hi, b_lo, o_ref):
    y = _dot3(c_hi[...], c_lo[...], a_hi[...], a_lo[...]) + _dot3(s_hi[...], s_lo[...], b_hi[...], b_lo[...])
    o_ref[...] = y.astype(o_ref.dtype)


def _fourier_seq(ab, tabs, n_seq, seq_len, seq_blk0, bl, bn):
    nr, nc = seq_len // bl, D_FOURIER // bn
    tab = pl.BlockSpec((bl, seq_len), lambda s, c, r: (r, 0))
    dat = pl.BlockSpec((seq_len, bn), lambda s, c, r: (seq_blk0 + s, c))
    return pl.pallas_call(
        _fourier_seq_body, grid=(n_seq, nc, nr),
        in_specs=[tab] * 4 + [dat] * 4,
        out_specs=pl.BlockSpec((bl, bn), lambda s, c, r: (s * nr + r, c)),
        out_shape=jax.ShapeDtypeStruct((n_seq * seq_len, D_FOURIER), BF16),
        compiler_params=_params(3), name=f"fourier_seq{seq_len}",
    )(*tabs, *ab)


CONV_ROWS = 2048


def _conv_body(b_ref, c_ref, h_ref, w_ref, o_ref):
    i = pl.program_id(0)
    seq_len = jnp.where(i * CONV_ROWS < N_PROMPT, SEQ, DEC_SEQ)
    u = c_ref[...] * h_ref[...]
    pos = lax.broadcasted_iota(jnp.int32, u.shape, 0) & (seq_len - 1)
    prev = jnp.where(pos == 0, 0.0, pltpu.roll(u, 1, 0))
    nxt = jnp.where(pos == seq_len - 1, 0.0, pltpu.roll(u, CONV_ROWS - 1, 0))
    w = w_ref[...]
    o_ref[...] = (b_ref[...] * (w[0:1] * prev + w[1:2] * u + w[2:3] * nxt)).astype(o_ref.dtype)


def _conv(z, conv_w, layer, bc=256):
    m = z.shape[0]
    nb = D_CONV // bc
    col0 = D_FOURIER // bc

    def part(k):
        return pl.BlockSpec((CONV_ROWS, bc), lambda i, j: (i, col0 + k * nb + j))

    return pl.pallas_call(
        _conv_body, grid=(m // CONV_ROWS, nb),
        in_specs=[part(0), part(1), part(2), pl.BlockSpec((None, 3, bc), lambda i, j: (layer, 0, j))],
        out_specs=pl.BlockSpec((CONV_ROWS, bc), lambda i, j: (i, j)),
        out_shape=jax.ShapeDtypeStruct((m, D_CONV), BF16),
        compiler_params=_params(2), name="short_conv",
    )(z, z, z, conv_w)


def _rope_k_body(x_ref, cos_ref, sin_ref, o_ref):
    x = x_ref[...]
    o_ref[...] = (x * cos_ref[...] + _rot_half_pairs(x) * sin_ref[...]).astype(o_ref.dtype)


def _rope_k(x, cos, sin, bm=512):
    m = x.shape[0]
    blk = pl.BlockSpec((bm, LANES), lambda i: (i, 0))
    return pl.pallas_call(
        _rope_k_body, grid=(m // bm,), in_specs=[blk, blk, blk], out_specs=blk,
        out_shape=jax.ShapeDtypeStruct((m, LANES), BF16),
        compiler_params=_params(1), name="rope_k",
    )(x, cos, sin)


def _attn_body(q_ref, kv_ref, kp_ref, o_ref, kcat_ref, *, heads):
    @pl.when(pl.program_id(2) == 0)
    def _():
        kp = kp_ref[...]
        for h in range(heads):
            kcat_ref[h, :, :QK_NOPE] = kv_ref[:, h * HEAD_PAD:h * HEAD_PAD + QK_NOPE]
            kcat_ref[h, :, QK_NOPE:] = kp

    for h in range(heads):
        q = q_ref[:, h * HEAD_PAD:(h + 1) * HEAD_PAD]
        v = kv_ref[:, h * HEAD_PAD + QK_NOPE:(h + 1) * HEAD_PAD]
        s = _dot_nt(q, kcat_ref[h]) * np.float32((QK_NOPE + QK_ROPE) ** -0.5)
        p = jnp.exp(s - jnp.max(s, axis=-1, keepdims=True))
        l = jnp.sum(p, axis=-1, keepdims=True)
        o_ref[:, h * V_HEAD:(h + 1) * V_HEAD] = (_dot(p.astype(BF16), v) / l).astype(o_ref.dtype)


def _attention(q, kv, kpe, n_seq, lq, lk, q_row0, k_row0, bq, heads):
    nq = lq // bq
    return pl.pallas_call(
        functools.partial(_attn_body, heads=heads), grid=(n_seq, N_HEADS // heads, nq),
        in_specs=[
            pl.BlockSpec((bq, heads * HEAD_PAD), lambda b, h, i: (q_row0 // bq + b * nq + i, h)),
            pl.BlockSpec((lk, heads * (QK_NOPE + V_HEAD)), lambda b, h, i: (k_row0 // lk + b, h)),
            pl.BlockSpec((lk, LANES), lambda b, h, i: (k_row0 // lk + b, 0)),
        ],
        out_specs=pl.BlockSpec((bq, heads * V_HEAD), lambda b, h, i: (b * nq + i, h)),
        out_shape=jax.ShapeDtypeStruct((n_seq * lq, N_HEADS * V_HEAD), BF16),
        scratch_shapes=[pltpu.VMEM((heads, lk, HEAD_PAD), BF16)],
        compiler_params=_params(3), name=f"attention{lq}",
    )(q, kv, kpe)


def _route(idx):
    flat_e = idx.reshape(-1)
    n = flat_e.shape[0]
    onehot = (flat_e[:, None] == jnp.arange(N_EXPERTS, dtype=jnp.int32)[None, :]).astype(jnp.int32)
    csum = jnp.cumsum(onehot, axis=0)
    rank = jnp.take_along_axis(csum, flat_e[:, None], axis=1)[:, 0] - 1
    counts = csum[-1]
    n_tile = (counts + MOE_BM - 1) // MOE_BM
    tile_end = jnp.cumsum(n_tile)
    tile_start = tile_end - n_tile
    pos = tile_start[flat_e] * MOE_BM + rank
    n_valid = tile_end[-1]
    t = jnp.arange(MOE_TILES, dtype=jnp.int32)
    tc = jnp.minimum(t, n_valid - 1)
    tile_e = jnp.sum((tile_end[None, :] <= tc[:, None]).astype(jnp.int32), axis=1)
    rows = jnp.clip(counts[tile_e] - (tc - tile_start[tile_e]) * MOE_BM, 0, MOE_BM)
    tile_rows = jnp.where(t < n_valid, rows, 0)
    src = jnp.zeros((MOE_TILES * MOE_BM,), jnp.int32).at[pos].set(jnp.arange(n, dtype=jnp.int32) // TOP_K)
    i32 = jnp.int32
    return pos.astype(i32), src, tile_e.astype(i32), tile_rows.astype(i32), n_valid.reshape(1).astype(i32)


GATHER_UNROLL = 8


def _gather_body(src_ref, nr_ref, h_hbm, o_ref, buf, sem):
    t, s = pl.program_id(0), pl.program_id(1)
    base = t * MOE_BM + s * GATHER_ROWS
    active = s * GATHER_ROWS < nr_ref[t]

    def row_copy(r, src_row):
        return pltpu.make_async_copy(h_hbm.at[pl.ds(src_row, 1)], buf.at[pl.ds(r, 1)], sem)

    @pl.when(active)
    def _():
        def start(g, c):
            for u in range(GATHER_UNROLL):
                r = g * GATHER_UNROLL + u
                row_copy(r, src_ref[base + r]).start()
            return c

        def wait(g, c):
            for u in range(GATHER_UNROLL):
                row_copy(g * GATHER_UNROLL + u, 0).wait()
            return c

        lax.fori_loop(0, GATHER_ROWS // GATHER_UNROLL, start, 0)
        lax.fori_loop(0, GATHER_ROWS // GATHER_UNROLL, wait, 0)
        o_ref[...] = buf[...].astype(o_ref.dtype)

    @pl.when(jnp.logical_not(active))
    def _():
        o_ref[...] = jnp.zeros_like(o_ref)


def _gather_rows(h, src, tile_rows):
    n_sub = MOE_BM // GATHER_ROWS
    grid_spec = pltpu.PrefetchScalarGridSpec(
        num_scalar_prefetch=2, grid=(MOE_TILES, n_sub),
        in_specs=[pl.BlockSpec(memory_space=pl.ANY)],
        out_specs=pl.BlockSpec((GATHER_ROWS, D_MODEL), lambda t, s, src, nr: (t * n_sub + s, 0)),
        scratch_shapes=[pltpu.VMEM((GATHER_ROWS, D_MODEL), F32), pltpu.SemaphoreType.DMA(())],
    )
    return pl.pallas_call(
        _gather_body, grid_spec=grid_spec,
        out_shape=jax.ShapeDtypeStruct((MOE_TILES * MOE_BM, D_MODEL), BF16),
        compiler_params=_params(2), name="moe_gather",
    )(src, tile_rows, h)


def _sub_blocks(nr_ref):
    return (nr_ref[pl.program_id(0)] + MOE_SUB - 1) // MOE_SUB


def _moe_up_body(te_ref, nr_ref, nv_ref, x_ref, w1_ref, w3_ref, o_ref):
    n_sub = _sub_blocks(nr_ref)
    for n in range(MOE_BM // MOE_SUB + 1):
        @pl.when(n_sub == n)
        def _(n=n):
            m = n * MOE_SUB
            if m:
                xv = x_ref[:m, :]
                a = _dot(xv, w1_ref[...].astype(BF16))
                b = _dot(xv, w3_ref[...].astype(BF16))
                o_ref[:m, :] = ((a * jax.nn.sigmoid(a)) * b).astype(o_ref.dtype)
            if m < MOE_BM:
                o_ref[m:, :] = jnp.zeros((MOE_BM - m, o_ref.shape[1]), o_ref.dtype)


def _moe_up(xs, w1, w3, moe_layer, tile_e, tile_rows, n_valid, bn=256):
    gj = D_FF_EXPERT // bn

    def row(t, nv):
        return jnp.minimum(t, nv[0] - 1)

    def col(t, j, nv):
        return jnp.where(t < nv[0], j, gj - 1)

    w_spec = pl.BlockSpec((None, None, D_MODEL, bn),
                          lambda t, j, te, nr, nv: (moe_layer, te[t], 0, col(t, j, nv)))
    grid_spec = pltpu.PrefetchScalarGridSpec(
        num_scalar_prefetch=3, grid=(MOE_TILES, gj),
        in_specs=[pl.BlockSpec((MOE_BM, D_MODEL), lambda t, j, te, nr, nv: (row(t, nv), 0)), w_spec, w_spec],
        out_specs=pl.BlockSpec((MOE_BM, bn), lambda t, j, te, nr, nv: (t, j)),
    )
    return pl.pallas_call(
        _moe_up_body, grid_spec=grid_spec,
        out_shape=jax.ShapeDtypeStruct((xs.shape[0], D_FF_EXPERT), BF16),
        compiler_params=_params(2), name="moe_up",
    )(tile_e, tile_rows, n_valid, xs, w1, w3)


def _moe_down_body(te_ref, nr_ref, nv_ref, x_ref, w_ref, o_ref):
    n_sub = _sub_blocks(nr_ref)
    first = pl.program_id(2) == 0
    for n in range(MOE_BM // MOE_SUB + 1):
        m = n * MOE_SUB

        @pl.when((n_sub == n) & first)
        def _(m=m):
            if m:
                o_ref[:m, :] = _dot(x_ref[:m, :], w_ref[...].astype(BF16))
            if m < MOE_BM:
                o_ref[m:, :] = jnp.zeros((MOE_BM - m, o_ref.shape[1]), o_ref.dtype)

        if m:
            @pl.when((n_sub == n) & jnp.logical_not(first))
            def _(m=m):
                o_ref[:m, :] += _dot(x_ref[:m, :], w_ref[...].astype(BF16))


def _moe_down(mid, w2, moe_layer, tile_e, tile_rows, n_valid, bn=1024, bk=2048):
    gj, gk = D_MODEL // bn, D_FF_EXPERT // bk

    def row(t, nv):
        return jnp.minimum(t, nv[0] - 1)

    def clamp(t, v, last, nv):
        return jnp.where(t < nv[0], v, last)

    grid_spec = pltpu.PrefetchScalarGridSpec(
        num_scalar_prefetch=3, grid=(MOE_TILES, gj, gk),
        in_specs=[
            pl.BlockSpec((MOE_BM, bk), lambda t, j, k, te, nr, nv: (row(t, nv), clamp(t, k, gk - 1, nv))),
            pl.BlockSpec((None, None, bk, bn),
                         lambda t, j, k, te, nr, nv: (moe_layer, te[t], clamp(t, k, gk - 1, nv),
                                                      clamp(t, j, gj - 1, nv))),
        ],
        out_specs=pl.BlockSpec((MOE_BM, bn), lambda t, j, k, te, nr, nv: (t, j)),
    )
    return pl.pallas_call(
        _moe_down_body, grid_spec=grid_spec,
        out_shape=jax.ShapeDtypeStruct((mid.shape[0], D_MODEL), F32),
        compiler_params=_params(3), name="moe_down",
    )(tile_e, tile_rows, n_valid, mid, w2)


def _combine_body(pos_ref, y_hbm, wt_ref, x_ref, g_ref, o_ref, buf, sem):
    base = pl.program_id(0) * COMBINE_ROWS

    def row_copy(r, c, src_row):
        return pltpu.make_async_copy(y_hbm.at[pl.ds(src_row, 1)], buf.at[c, pl.ds(r, 1)], sem)

    def start(g, carry):
        for u in range(GATHER_UNROLL):
            r = g * GATHER_UNROLL + u
            for c in range(TOP_K):
                row_copy(r, c, pos_ref[(base + r) * TOP_K + c]).start()
        return carry

    def wait(g, carry):
        for u in range(GATHER_UNROLL):
            for c in range(TOP_K):
                row_copy(g * GATHER_UNROLL + u, c, 0).wait()
        return carry

    lax.fori_loop(0, COMBINE_ROWS // GATHER_UNROLL, start, 0)
    lax.fori_loop(0, COMBINE_ROWS // GATHER_UNROLL, wait, 0)
    g = g_ref[...]
    for r0 in range(0, COMBINE_ROWS, LANES):
        rows = pl.ds(r0, LANES)
        wt = wt_ref[rows, :]
        f = wt[:, 0:1] * buf[0, rows, :] + wt[:, 1:2] * buf[1, rows, :]
        o_ref[rows, :] = x_ref[rows, :] + g * f


def _moe_combine(y, pos, wts, x, mods, k_gate):
    m = x.shape[0]
    row = pl.BlockSpec((COMBINE_ROWS, D_MODEL), lambda i, p: (i, 0))
    grid_spec = pltpu.PrefetchScalarGridSpec(
        num_scalar_prefetch=1, grid=(m // COMBINE_ROWS,),
        in_specs=[
            pl.BlockSpec(memory_space=pl.ANY),
            pl.BlockSpec((COMBINE_ROWS, LANES), lambda i, p: (i, 0)),
            row,
            pl.BlockSpec((None, 1, D_MODEL), lambda i, p: (_cond_of_tile(i, COMBINE_ROWS) * 6 + k_gate, 0, 0)),
        ],
        out_specs=row,
        scratch_shapes=[pltpu.VMEM((TOP_K, COMBINE_ROWS, D_MODEL), F32), pltpu.SemaphoreType.DMA(())],
    )
    return pl.pallas_call(
        _combine_body, grid_spec=grid_spec,
        out_shape=jax.ShapeDtypeStruct((m, D_MODEL), F32),
        compiler_params=_params(1), name="moe_combine",
    )(pos, y, wts, x, mods)


def _rope_tables():
    n_rows = DEC_SEQ // GRID_W
    row = jnp.repeat(jnp.arange(n_rows), GRID_W).astype(F32)
    col = jnp.tile(jnp.arange(GRID_W), n_rows).astype(F32)
    half = QK_ROPE // 2
    inv = ROPE_BASE ** (-(jnp.arange(half // 2, dtype=F32) * 2.0) / half)
    ang_r = row[:, None] * inv
    ang_c = col[:, None] * inv
    ang = jnp.concatenate([ang_r, ang_r, ang_c, ang_c], axis=-1)
    pad1 = jnp.ones((DEC_SEQ, LANES - QK_ROPE), F32)
    cos = jnp.concatenate([jnp.cos(ang), pad1], axis=1)
    sin = jnp.concatenate([jnp.sin(ang), 0 * pad1], axis=1)
    return cos, sin


def _mod_spec(bm, bn, which):
    return ((None, 1, bn), lambda i, j: (_cond_of_tile(i, bm) * 6 + which, 0, j))


def kernel(x_prompt, x_sample, cache_ckv, cache_kpe, c, c_ctx, norm1_g, norm2_g, w_ada, b_ada, w_in,
           w_fourier_out, conv_w, w_conv_out, q_norm_g, w_uq, kv_norm_g, w_ukv, w_mla_out, w_o,
           ffn_w1, ffn_w3, ffn_w2, router_w, router_b, moe_w1, moe_w3, moe_w2, final_g):
    bm = 1024
    x = jnp.concatenate([x_prompt.reshape(N_PROMPT, D_MODEL), x_sample.reshape(N_SAMPLE, D_MODEL)], axis=0)
    w_in_t = jnp.swapaxes(w_in, 1, 2)

    cond8 = jnp.concatenate([c_ctx[None, :], c, jnp.zeros((8 - 1 - DEC_BATCH, D_MODEL), F32)], axis=0)
    mods_all = _ada(cond8, w_ada, b_ada)[:, :1 + DEC_BATCH].reshape(DEPTH, (1 + DEC_BATCH) * 6, 1, D_MODEL)

    cos, sin = _rope_tables()
    one = jnp.ones((N_PROMPT, LANES), F32)
    cos_q = jnp.concatenate([one] + [cos] * DEC_BATCH, axis=0)
    sin_q = jnp.concatenate([0 * one] + [sin] * DEC_BATCH, axis=0)
    one_p = jnp.ones((PAST_LEN, LANES), F32)
    cos_k = jnp.concatenate([one_p, cos] * DEC_BATCH + [one], axis=0)
    sin_k = jnp.concatenate([0 * one_p, sin] * DEC_BATCH + [0 * one], axis=0)
    tab_chan = _dft_tables(FOURIER_GROUP, 1.0)
    tab_prompt = _dft_tables(SEQ, -1.0)
    tab_sample = _dft_tables(DEC_SEQ, -1.0)

    ckv_out, kpe_out = [], []
    for layer in range(DEPTH):
        mods = mods_all[layer]
        uq = w_uq[layer].reshape(Q_LORA, N_HEADS, QK_NOPE + QK_ROPE)
        uq = jnp.concatenate([uq, jnp.zeros((Q_LORA, N_HEADS, HEAD_PAD - QK_NOPE - QK_ROPE), F32)], axis=-1)
        uq = uq.reshape(Q_LORA, N_HEADS * HEAD_PAD).astype(BF16)

        h = _norm_mod(x, norm1_g, layer, mods, 1, 0)
        z = _mm("in_proj", h, [(w_in_t, (layer,), 0, True)], _ep_plain, bm=bm, bn=512, n_out=Z_MAIN, out_dtype=F32)

        ab = _fourier_chan(z, tab_chan)
        yf_pre = jnp.concatenate([
            _fourier_seq(ab, tab_prompt, BATCH, SEQ, 0, SEQ, D_FOURIER),
            _fourier_seq(ab, tab_sample, DEC_BATCH, DEC_SEQ, N_PROMPT // DEC_SEQ, 512, 512)], axis=0)
        y_f = _mm("fourier_out", yf_pre, [(w_fourier_out, (layer,), 0, False)], _ep_plain,
                  bm=bm, bn=2048, n_out=D_MODEL, out_dtype=F32)

        yc_pre = _conv(z, conv_w, layer)
        y_c = _mm("conv_out", yc_pre, [(w_conv_out, (layer,), 0, False)], _ep_plain,
                  bm=bm, bn=2048, n_out=D_MODEL, out_dtype=F32)

        q_col = (D_FOURIER + 3 * D_CONV) // Q_LORA
        q = _mm("q_proj", z, [(uq, (), 0, False)], _ep_rope_q, bm=bm, bn=1024, n_out=N_HEADS * HEAD_PAD,
                out_dtype=BF16, k=Q_LORA, x_kblk=q_col, prologue=_pro_rmsnorm,
                pro_extras=[(q_norm_g.reshape(DEPTH, 1, Q_LORA), (None, 1, Q_LORA), lambda i, j: (layer, 0, 0))],
                extras=[(cos_q, (bm, LANES), lambda i, j: (i, 0)), (sin_q, (bm, LANES), lambda i, j: (i, 0))])
        kv_col = (D_FOURIER + 3 * D_CONV + Q_LORA) // KV_LORA
        c_kv = _rmsnorm(z, kv_norm_g.reshape(DEPTH, 1, KV_LORA), layer, KV_LORA, kv_col)
        k_pe = z[:, GATE_COL0 - QK_ROPE:GATE_COL0]
        ckv_out.append(c_kv[:N_PROMPT].reshape(BATCH, SEQ, KV_LORA))
        kpe_out.append(k_pe[:N_PROMPT].reshape(BATCH, SEQ, QK_ROPE))
        ckv_s = jnp.concatenate([cache_ckv[:, layer], c_kv[N_PROMPT:].reshape(DEC_BATCH, DEC_SEQ, KV_LORA)], axis=1)
        ckv_ext = jnp.concatenate([ckv_s.reshape(-1, KV_LORA), c_kv[:N_PROMPT]], axis=0).astype(BF16)
        kpe_s = jnp.concatenate([cache_kpe[:, layer], k_pe[N_PROMPT:].reshape(DEC_BATCH, DEC_SEQ, QK_ROPE)], axis=1)
        kpe_ext = jnp.concatenate([kpe_s.reshape(-1, QK_ROPE), k_pe[:N_PROMPT]], axis=0)
        kpe_ext = jnp.concatenate([kpe_ext, jnp.zeros((N_KV_ROWS, LANES - QK_ROPE), F32)], axis=1)
        kpe_ext = _rope_k(kpe_ext, cos_k, sin_k)
        kv = _mm("kv_proj", ckv_ext, [(w_ukv, (layer,), 0, False)], _ep_plain,
                 bm=512, bn=2048, n_out=N_HEADS * (QK_NOPE + V_HEAD), out_dtype=BF16)
        attn = jnp.concatenate([
            _attention(q, kv, kpe_ext, BATCH, SEQ, SEQ, 0, DEC_BATCH * KV_LEN, SEQ, N_HEADS),
            _attention(q, kv, kpe_ext, DEC_BATCH, DEC_SEQ, KV_LEN, N_PROMPT, 0, 512, 8)], axis=0)
        y_a = _mm("mla_out", attn, [(w_mla_out, (layer,), 0, False)], _ep_plain,
                  bm=bm, bn=1024, n_out=D_MODEL, out_dtype=F32)

        tile = ((bm, 256), lambda i, j: (i, j))
        merged = _mm("gate_merge", h,
                     [(w_in_t, (layer,), GATE_COL0 + b * D_MODEL, True) for b in range(N_BRANCH)], _ep_merge,
                     bm=bm, bn=256, n_out=D_MODEL, out_dtype=BF16, x_buffers=1,
                     extras=[(y_f,) + tile, (y_c,) + tile, (y_a,) + tile])
        x = _mm("out_proj", merged, [(w_o, (layer,), 0, False)], _ep_residual, bm=bm, bn=512, n_out=D_MODEL,
                out_dtype=F32, extras=[(x, (bm, 512), lambda i, j: (i, j)), (mods,) + _mod_spec(bm, 512, 2)])

        j = layer // 2
        if layer % 2 == 0:
            h2 = _norm_mod(x, norm2_g, layer, mods, 4, 3)
            mid = _mm("ffn_up", h2, [(ffn_w1, (j,), 0, False), (ffn_w3, (j,), 0, False)], _ep_swiglu,
                      bm=bm, bn=256, n_out=D_FF, out_dtype=BF16)
            x = _mm("ffn_down", mid, [(ffn_w2[j].astype(BF16), (), 0, False)], _ep_residual, bm=512, bn=256,
                    n_out=D_MODEL, out_dtype=F32,
                    extras=[(x, (512, 256), lambda i, j: (i, j)), (mods,) + _mod_spec(512, 256, 5)])
        else:
            rw = jnp.concatenate([router_w[j], jnp.zeros((D_MODEL, LANES - N_EXPERTS), F32)], axis=1)
            rb = jnp.concatenate([router_b[j], jnp.zeros((LANES - N_EXPERTS,), F32)])[None, :]
            h2f, wts, idx = _norm_mod(x, norm2_g, layer, mods, 4, 3, router=_split_bf16(rw) + (rb,))
            pos, src, tile_e, tile_rows, n_valid = _route(idx[:, :TOP_K])
            xs = _gather_rows(h2f, src, tile_rows)
            mid = _moe_up(xs, moe_w1, moe_w3, j, tile_e, tile_rows, n_valid)
            y = _moe_down(mid, moe_w2, j, tile_e, tile_rows, n_valid)
            x = _moe_combine(y, pos, wts, x, mods, 5)

    fg = final_g.reshape(1, 1, D_MODEL)
    y_prompt = _rmsnorm(x, fg, 0, D_MODEL, 0, bm=256, row0=0, rows=N_PROMPT)
    y_sample = _rmsnorm(x, fg, 0, D_MODEL, 0, bm=256, row0=N_PROMPT, rows=N_SAMPLE)
    return (y_prompt.reshape(BATCH, SEQ, D_MODEL),
            y_sample.reshape(DEC_BATCH, DEC_SEQ, D_MODEL),
            jnp.stack(ckv_out, axis=1),
            jnp.stack(kpe_out, axis=1))
```

```python
import functools

import jax
import jax.numpy as jnp
import numpy as np
from jax import lax
from jax.experimental import pallas as pl
from jax.experimental.pallas import tpu as pltpu

F32 = jnp.float32
BF16 = jnp.bfloat16

D_MODEL = 4096
BATCH = 16
SEQ = 256
DEPTH = 2
DEC_BATCH = 2
DEC_SEQ = 2048
PAST_LEN = 256
GRID_W = 64
N_FOURIER_GROUPS = 4
FOURIER_GROUP = 256
D_FOURIER = N_FOURIER_GROUPS * FOURIER_GROUP
D_CONV = 1024
N_HEADS = 16
Q_LORA = 1024
KV_LORA = 512
QK_NOPE = 128
QK_ROPE = 64
V_HEAD = 128
ROPE_BASE = 10000.0
N_BRANCH = 3
D_FF = 11008
N_EXPERTS = 8
TOP_K = 2
D_FF_EXPERT = 14336
EPS = 1e-6

N_PROMPT = BATCH * SEQ
N_SAMPLE = DEC_BATCH * DEC_SEQ
N_TOK = N_PROMPT + N_SAMPLE
KV_LEN = PAST_LEN + DEC_SEQ
N_KV_ROWS = DEC_BATCH * KV_LEN + N_PROMPT
GATE_COL0 = D_FOURIER + 3 * D_CONV + Q_LORA + KV_LORA + QK_ROPE
Z_MAIN = 6144
HEAD_PAD = 256
LANES = 128

VMEM_LIMIT = 56 * 1024 * 1024
MOE_SUB = 256
MOE_BM = 5 * MOE_SUB
MOE_TILES = (TOP_K * N_TOK) // MOE_BM + N_EXPERTS
GATHER_ROWS = 256


def _params(n_axes, vmem=VMEM_LIMIT):
    return pltpu.CompilerParams(dimension_semantics=("arbitrary",) * n_axes, vmem_limit_bytes=vmem)


def _cond_of_tile(i, bm):
    r0 = i * bm
    return jnp.where(r0 < N_PROMPT, 0, 1 + (r0 - N_PROMPT) // DEC_SEQ)


def _split_bf16(x):
    hi = x.astype(BF16)
    lo = (x - hi.astype(F32)).astype(BF16)
    return hi, lo


def _dot(a, b):
    return jnp.dot(a, b, preferred_element_type=F32)


def _dot3(a_hi, a_lo, b_hi, b_lo):
    return _dot(a_hi, b_hi) + _dot(a_hi, b_lo) + _dot(a_lo, b_hi)


def _ada_body(c_ref, w_ref, b_ref, o_ref):
    c = c_ref[...]
    s = (c * jax.nn.sigmoid(c)).astype(BF16)
    o_ref[...] = _dot(s, w_ref[...].astype(BF16)) + b_ref[...]


def _ada(cond8, w_ada, b_ada, bn=512):
    n = w_ada.shape[-1]
    return pl.pallas_call(
        _ada_body,
        grid=(DEPTH, n // bn),
        in_specs=[
            pl.BlockSpec((8, D_MODEL), lambda l, j: (0, 0)),
            pl.BlockSpec((None, D_MODEL, bn), lambda l, j: (l, 0, j)),
            pl.BlockSpec((None, 1, bn), lambda l, j: (l, 0, j)),
        ],
        out_specs=pl.BlockSpec((None, 8, bn), lambda l, j: (l, 0, j)),
        out_shape=jax.ShapeDtypeStruct((DEPTH, 8, n), F32),
        compiler_params=_params(2),
        name="ada",
    )(cond8, w_ada, b_ada.reshape(DEPTH, 1, n))


def _rms(x):
    return x * lax.rsqrt(jnp.mean(x * x, axis=-1, keepdims=True) + EPS)


def _norm_mod_body(x_ref, g_ref, sc_ref, sh_ref, o_ref):
    y = _rms(x_ref[...]) * g_ref[...]
    o_ref[...] = (y * (1 + sc_ref[...]) + sh_ref[...]).astype(o_ref.dtype)


def _norm_mod_router_body(x_ref, g_ref, sc_ref, sh_ref, rw_hi_ref, rw_lo_ref, rb_ref,
                          of_ref, wt_ref, ix_ref):
    y = _rms(x_ref[...]) * g_ref[...]
    h = y * (1 + sc_ref[...]) + sh_ref[...]
    of_ref[...] = h
    h_hi, h_lo = _split_bf16(h)
    logits = _dot3(h_hi, h_lo, rw_hi_ref[...], rw_lo_ref[...]) + rb_ref[...]
    lane = lax.broadcasted_iota(jnp.int32, logits.shape, 1)
    lanef = lane.astype(F32)
    neg = jnp.float32(-jnp.inf)
    lg = jnp.where(lane < N_EXPERTS, logits, neg)
    m1 = jnp.max(lg, axis=-1, keepdims=True)
    i1 = jnp.min(jnp.where(lg == m1, lanef, float(LANES)), axis=-1, keepdims=True)
    lg2 = jnp.where(lanef == i1, neg, lg)
    m2 = jnp.max(lg2, axis=-1, keepdims=True)
    i2 = jnp.min(jnp.where(lg2 == m2, lanef, float(LANES)), axis=-1, keepdims=True)
    e = jnp.exp(m2 - m1)
    w1 = 1.0 / (1.0 + e)
    w2 = e / (1.0 + e)
    wt_ref[...] = jnp.where(lane == 0, w1, jnp.where(lane == 1, w2, 0.0))
    ix_ref[...] = jnp.where(lane == 0, i1, jnp.where(lane == 1, i2, 0.0)).astype(jnp.int32)


def _norm_mod(x, gain, layer, mods, k_scale, k_shift, router=None, bm=256):
    m = x.shape[0]
    in_specs = [
        pl.BlockSpec((bm, D_MODEL), lambda i: (i, 0)),
        pl.BlockSpec((None, 1, D_MODEL), lambda i: (layer, 0, 0)),
        pl.BlockSpec((None, 1, D_MODEL), lambda i: (_cond_of_tile(i, bm) * 6 + k_scale, 0, 0)),
        pl.BlockSpec((None, 1, D_MODEL), lambda i: (_cond_of_tile(i, bm) * 6 + k_shift, 0, 0)),
    ]
    args = [x, gain.reshape(DEPTH, 1, D_MODEL), mods, mods]
    row = pl.BlockSpec((bm, D_MODEL), lambda i: (i, 0))
    if router is None:
        return pl.pallas_call(
            _norm_mod_body, grid=(m // bm,), in_specs=in_specs, out_specs=row,
            out_shape=jax.ShapeDtypeStruct((m, D_MODEL), BF16),
            compiler_params=_params(1), name="norm_mod",
        )(*args)
    rw_hi, rw_lo, rb = router
    small = pl.BlockSpec((bm, LANES), lambda i: (i, 0))
    in_specs += [
        pl.BlockSpec((D_MODEL, LANES), lambda i: (0, 0)),
        pl.BlockSpec((D_MODEL, LANES), lambda i: (0, 0)),
        pl.BlockSpec((1, LANES), lambda i: (0, 0)),
    ]
    return pl.pallas_call(
        _norm_mod_router_body, grid=(m // bm,), in_specs=in_specs,
        out_specs=[row, small, small],
        out_shape=[jax.ShapeDtypeStruct((m, D_MODEL), F32),
                   jax.ShapeDtypeStruct((m, LANES), F32), jax.ShapeDtypeStruct((m, LANES), jnp.int32)],
        compiler_params=_params(1), name="norm_mod_router",
    )(*args, rw_hi, rw_lo, rb)


def _rmsnorm_body(x_ref, g_ref, o_ref):
    o_ref[...] = (_rms(x_ref[...]) * g_ref[...]).astype(o_ref.dtype)


def _rmsnorm(x, gain3, lead, width, col_blk, bm=512, row0=0, rows=None):
    m = rows or x.shape[0]
    return pl.pallas_call(
        _rmsnorm_body, grid=(m // bm,),
        in_specs=[pl.BlockSpec((bm, width), lambda i: (row0 // bm + i, col_blk)),
                  pl.BlockSpec((None, 1, width), lambda i: (lead, 0, 0))],
        out_specs=pl.BlockSpec((bm, width), lambda i: (i, 0)),
        out_shape=jax.ShapeDtypeStruct((m, width), F32),
        compiler_params=_params(1), name="rmsnorm",
    )(x, gain3)


def _dot_nt(a, b):
    return lax.dot_general(a, b, (((1,), (1,)), ((), ())), preferred_element_type=F32)


def _mm(name, x, ws, epilogue, *, bm, bn, n_out, out_dtype, k=None, x_kblk=0, extras=(),
        prologue=None, pro_extras=(), x_buffers=2):
    m = x.shape[0]
    k = k or x.shape[1]
    n_w, n_e, n_p = len(ws), len(extras), len(pro_extras)
    x_mode = {} if x_buffers == 2 else {"pipeline_mode": pl.Buffered(x_buffers)}
    in_specs = [pl.BlockSpec((bm, k), lambda i, j: (i, x_kblk), **x_mode)]
    args = [x]
    for arr, blk, imap in pro_extras:
        in_specs.append(pl.BlockSpec(blk, imap))
        args.append(arr)
    for arr, lead, col0, transposed in ws:
        squeeze = (None,) * len(lead)
        if transposed:
            in_specs.append(pl.BlockSpec((pl.Element(1),) * len(lead) + (pl.Element(bn), pl.Element(k)),
                                         lambda i, j, lead=lead, col0=col0:
                                         lead + (pl.multiple_of(col0 + j * bn, 8), 0)))
        else:
            assert col0 % bn == 0
            in_specs.append(pl.BlockSpec(squeeze + (k, bn),
                                         lambda i, j, lead=lead, col0=col0: lead + (0, j + col0 // bn)))
        args.append(arr)
    for arr, blk, imap in extras:
        in_specs.append(pl.BlockSpec(blk, imap))
        args.append(arr)

    def body(*refs):
        x_ref = refs[0]
        p_refs = refs[1:1 + n_p]
        w_refs = refs[1 + n_p:1 + n_p + n_w]
        e_refs = refs[1 + n_p + n_w:1 + n_p + n_w + n_e]
        o_ref = refs[1 + n_p + n_w + n_e]
        if prologue is None:
            xv = x_ref[...]
        else:
            xs_ref = refs[-1]

            @pl.when(pl.program_id(1) == 0)
            def _():
                xs_ref[...] = prologue(x_ref, *p_refs)

            xv = xs_ref[...]
        accs = []
        for (_, lead, _, transposed), w_ref in zip(ws, w_refs):
            if transposed:
                accs.append(_dot_nt(xv, w_ref[(0,) * len(lead)].astype(BF16)))
            else:
                accs.append(_dot(xv, w_ref[...].astype(BF16)))
        o_ref[...] = epilogue(accs, *e_refs).astype(o_ref.dtype)

    scratch = [] if prologue is None else [pltpu.VMEM((bm, k), BF16)]
    return pl.pallas_call(
        body, grid=(m // bm, pl.cdiv(n_out, bn)), in_specs=in_specs,
        out_specs=pl.BlockSpec((bm, bn), lambda i, j: (i, j)),
        out_shape=jax.ShapeDtypeStruct((m, n_out), out_dtype),
        scratch_shapes=scratch, compiler_params=_params(2), name=name,
    )(*args)


def _ep_plain(accs):
    return accs[0]


def _ep_swiglu(accs):
    a, b = accs
    return (a * jax.nn.sigmoid(a)) * b


def _ep_residual(accs, x_ref, g_ref):
    return x_ref[...] + g_ref[...] * accs[0]


def _ep_merge(accs, yf_ref, yc_ref, ya_ref):
    g0, g1, g2 = (jax.nn.sigmoid(a) for a in accs)
    return g0 * yf_ref[...] + g1 * yc_ref[...] + g2 * ya_ref[...]


def _rot_half_pairs(x):
    lane = lax.broadcasted_iota(jnp.int32, x.shape, 1)
    first = (lane // 16) % 2 == 0
    return jnp.where(first, -pltpu.roll(x, LANES - 16, 1), pltpu.roll(x, 16, 1))


def _ep_rope_q(accs, cos_ref, sin_ref):
    acc = accs[0]
    cos, sin = cos_ref[...], sin_ref[...]
    parts = []
    for h in range(acc.shape[1] // HEAD_PAD):
        base = h * HEAD_PAD
        parts.append(acc[:, base:base + QK_NOPE])
        pe = acc[:, base + QK_NOPE:base + HEAD_PAD]
        parts.append(pe * cos + _rot_half_pairs(pe) * sin)
    return jnp.concatenate(parts, axis=1)


def _pro_rmsnorm(x_ref, g_ref):
    return (_rms(x_ref[...]) * g_ref[...]).astype(BF16)


def _dft_tables(n, sign):
    p = jnp.arange(n, dtype=jnp.int32)
    ang = ((p[:, None] * p[None, :]) % n).astype(F32) * np.float32(2.0 * np.pi / n)
    scale = np.float32(1.0 / np.sqrt(n))
    return _split_bf16(jnp.cos(ang) * scale) + _split_bf16(jnp.sin(ang) * (sign * scale))


def _fourier_chan_body(u_ref, c_hi, c_lo, s_hi, s_lo, a_hi, a_lo, b_hi, b_lo):
    for g in range(N_FOURIER_GROUPS):
        sl = slice(g * FOURIER_GROUP, (g + 1) * FOURIER_GROUP)
        u_h, u_l = _split_bf16(u_ref[:, sl])
        ah, al = _split_bf16(_dot3(u_h, u_l, c_hi[...], c_lo[...]))
        bh, bl = _split_bf16(_dot3(u_h, u_l, s_hi[...], s_lo[...]))
        a_hi[:, sl] = ah
        a_lo[:, sl] = al
        b_hi[:, sl] = bh
        b_lo[:, sl] = bl


def _fourier_chan(z, tabs, bm=512):
    m = z.shape[0]
    tab = pl.BlockSpec((FOURIER_GROUP, FOURIER_GROUP), lambda i: (0, 0))
    row = pl.BlockSpec((bm, D_FOURIER), lambda i: (i, 0))
    return pl.pallas_call(
        _fourier_chan_body, grid=(m // bm,),
        in_specs=[row, tab, tab, tab, tab], out_specs=[row] * 4,
        out_shape=[jax.ShapeDtypeStruct((m, D_FOURIER), BF16)] * 4,
        compiler_params=_params(1), name="fourier_chan",
    )(z, *tabs)


def _fourier_seq_body(c_hi, c_lo, s_hi, s_lo, a_hi, a_lo, b_hi, b_lo, o_ref):
    y = _dot3(c_hi[...], c_lo[...], a_hi[...], a_lo[...]) + _dot3(s_hi[...], s_lo[...], b_hi[...], b_lo[...])
    o_ref[...] = y.astype(o_ref.dtype)


def _fourier_seq(ab, tabs, n_seq, seq_len, seq_blk0, bl, bn):
    nr, nc = seq_len // bl, D_FOURIER // bn
    tab = pl.BlockSpec((bl, seq_len), lambda s, c, r: (r, 0))
    dat = pl.BlockSpec((seq_len, bn), lambda s, c, r: (seq_blk0 + s, c))
    return pl.pallas_call(
        _fourier_seq_body, grid=(n_seq, nc, nr),
        in_specs=[tab] * 4 + [dat] * 4,
        out_specs=pl.BlockSpec((bl, bn), lambda s, c, r: (s * nr + r, c)),
        out_shape=jax.ShapeDtypeStruct((n_seq * seq_len, D_FOURIER), BF16),
        compiler_params=_params(3), name=f"fourier_seq{seq_len}",
    )(*tabs, *ab)


CONV_ROWS = 2048


def _conv_body(b_ref, c_ref, h_ref, w_ref, o_ref):
    i = pl.program_id(0)
    seq_len = jnp.where(i * CONV_ROWS < N_PROMPT, SEQ, DEC_SEQ)
    u = c_ref[...] * h_ref[...]
    pos = lax.broadcasted_iota(jnp.int32, u.shape, 0) & (seq_len - 1)
    prev = jnp.where(pos == 0, 0.0, pltpu.roll(u, 1, 0))
    nxt = jnp.where(pos == seq_len - 1, 0.0, pltpu.roll(u, CONV_ROWS - 1, 0))
    w = w_ref[...]
    o_ref[...] = (b_ref[...] * (w[0:1] * prev + w[1:2] * u + w[2:3] * nxt)).astype(o_ref.dtype)


def _conv(z, conv_w, layer, bc=256):
    m = z.shape[0]
    nb = D_CONV // bc
    col0 = D_FOURIER // bc

    def part(k):
        return pl.BlockSpec((CONV_ROWS, bc), lambda i, j: (i, col0 + k * nb + j))

    return pl.pallas_call(
        _conv_body, grid=(m // CONV_ROWS, nb),
        in_specs=[part(0), part(1), part(2), pl.BlockSpec((None, 3, bc), lambda i, j: (layer, 0, j))],
        out_specs=pl.BlockSpec((CONV_ROWS, bc), lambda i, j: (i, j)),
        out_shape=jax.ShapeDtypeStruct((m, D_CONV), BF16),
        compiler_params=_params(2), name="short_conv",
    )(z, z, z, conv_w)


def _rope_k_body(x_ref, cos_ref, sin_ref, o_ref):
    x = x_ref[...]
    o_ref[...] = (x * cos_ref[...] + _rot_half_pairs(x) * sin_ref[...]).astype(o_ref.dtype)


def _rope_k(x, cos, sin, bm=512):
    m = x.shape[0]
    blk = pl.BlockSpec((bm, LANES), lambda i: (i, 0))
    return pl.pallas_call(
        _rope_k_body, grid=(m // bm,), in_specs=[blk, blk, blk], out_specs=blk,
        out_shape=jax.ShapeDtypeStruct((m, LANES), BF16),
        compiler_params=_params(1), name="rope_k",
    )(x, cos, sin)


def _attn_body(q_ref, kv_ref, kp_ref, o_ref, kcat_ref, *, heads):
    @pl.when(pl.program_id(2) == 0)
    def _():
        kp = kp_ref[...]
        for h in range(heads):
            kcat_ref[h, :, :QK_NOPE] = kv_ref[:, h * HEAD_PAD:h * HEAD_PAD + QK_NOPE]
            kcat_ref[h, :, QK_NOPE:] = kp

    for h in range(heads):
        q = q_ref[:, h * HEAD_PAD:(h + 1) * HEAD_PAD]
        v = kv_ref[:, h * HEAD_PAD + QK_NOPE:(h + 1) * HEAD_PAD]
        s = _dot_nt(q, kcat_ref[h]) * np.float32((QK_NOPE + QK_ROPE) ** -0.5)
        p = jnp.exp(s - jnp.max(s, axis=-1, keepdims=True))
        l = jnp.sum(p, axis=-1, keepdims=True)
        o_ref[:, h * V_HEAD:(h + 1) * V_HEAD] = (_dot(p.astype(BF16), v) / l).astype(o_ref.dtype)


def _attention(q, kv, kpe, n_seq, lq, lk, q_row0, k_row0, bq, heads):
    nq = lq // bq
    return pl.pallas_call(
        functools.partial(_attn_body, heads=heads), grid=(n_seq, N_HEADS // heads, nq),
        in_specs=[
            pl.BlockSpec((bq, heads * HEAD_PAD), lambda b, h, i: (q_row0 // bq + b * nq + i, h)),
            pl.BlockSpec((lk, heads * (QK_NOPE + V_HEAD)), lambda b, h, i: (k_row0 // lk + b, h)),
            pl.BlockSpec((lk, LANES), lambda b, h, i: (k_row0 // lk + b, 0)),
        ],
        out_specs=pl.BlockSpec((bq, heads * V_HEAD), lambda b, h, i: (b * nq + i, h)),
        out_shape=jax.ShapeDtypeStruct((n_seq * lq, N_HEADS * V_HEAD), BF16),
        scratch_shapes=[pltpu.VMEM((heads, lk, HEAD_PAD), BF16)],
        compiler_params=_params(3), name=f"attention{lq}",
    )(q, kv, kpe)


def _route(idx):
    flat_e = idx.reshape(-1)
    n = flat_e.shape[0]
    onehot = (flat_e[:, None] == jnp.arange(N_EXPERTS, dtype=jnp.int32)[None, :]).astype(jnp.int32)
    csum = jnp.cumsum(onehot, axis=0)
    rank = jnp.take_along_axis(csum, flat_e[:, None], axis=1)[:, 0] - 1
    counts = csum[-1]
    n_tile = (counts + MOE_BM - 1) // MOE_BM
    tile_end = jnp.cumsum(n_tile)
    tile_start = tile_end - n_tile
    pos = tile_start[flat_e] * MOE_BM + rank
    n_valid = tile_end[-1]
    t = jnp.arange(MOE_TILES, dtype=jnp.int32)
    tc = jnp.minimum(t, n_valid - 1)
    tile_e = jnp.sum((tile_end[None, :] <= tc[:, None]).astype(jnp.int32), axis=1)
    rows = jnp.clip(counts[tile_e] - (tc - tile_start[tile_e]) * MOE_BM, 0, MOE_BM)
    tile_rows = jnp.where(t < n_valid, rows, 0)
    src = jnp.zeros((MOE_TILES * MOE_BM,), jnp.int32).at[pos].set(jnp.arange(n, dtype=jnp.int32) // TOP_K)
    i32 = jnp.int32
    return pos.astype(i32), src, tile_e.astype(i32), tile_rows.astype(i32), n_valid.reshape(1).astype(i32)


GATHER_UNROLL = 8


def _gather_body(src_ref, nr_ref, h_hbm, o_ref, buf, sem):
    t, s = pl.program_id(0), pl.program_id(1)
    base = t * MOE_BM + s * GATHER_ROWS
    active = s * GATHER_ROWS < nr_ref[t]

    def row_copy(r, src_row):
        return pltpu.make_async_copy(h_hbm.at[pl.ds(src_row, 1)], buf.at[pl.ds(r, 1)], sem)

    @pl.when(active)
    def _():
        def start(g, c):
            for u in range(GATHER_UNROLL):
                r = g * GATHER_UNROLL + u
                row_copy(r, src_ref[base + r]).start(priority=u % 2)
            return c

        def wait(g, c):
            for u in range(GATHER_UNROLL):
                row_copy(g * GATHER_UNROLL + u, 0).wait()
            return c

        lax.fori_loop(0, GATHER_ROWS // GATHER_UNROLL, start, 0)
        lax.fori_loop(0, GATHER_ROWS // GATHER_UNROLL, wait, 0)
        o_ref[...] = buf[...].astype(o_ref.dtype)

    @pl.when(jnp.logical_not(active))
    def _():
        o_ref[...] = jnp.zeros_like(o_ref)


def _gather_rows(h, src, tile_rows):
    n_sub = MOE_BM // GATHER_ROWS
    grid_spec = pltpu.PrefetchScalarGridSpec(
        num_scalar_prefetch=2, grid=(MOE_TILES, n_sub),
        in_specs=[pl.BlockSpec(memory_space=pl.ANY)],
        out_specs=pl.BlockSpec((GATHER_ROWS, D_MODEL), lambda t, s, src, nr: (t * n_sub + s, 0)),
        scratch_shapes=[pltpu.VMEM((GATHER_ROWS, D_MODEL), F32), pltpu.SemaphoreType.DMA(())],
    )
    return pl.pallas_call(
        _gather_body, grid_spec=grid_spec,
        out_shape=jax.ShapeDtypeStruct((MOE_TILES * MOE_BM, D_MODEL), BF16),
        compiler_params=_params(2), name="moe_gather",
    )(src, tile_rows, h)


def _sub_blocks(nr_ref):
    return (nr_ref[pl.program_id(0)] + MOE_SUB - 1) // MOE_SUB


def _moe_up_body(te_ref, nr_ref, nv_ref, x_ref, w1_ref, w3_ref, o_ref):
    n_sub = _sub_blocks(nr_ref)
    for n in range(MOE_BM // MOE_SUB + 1):
        @pl.when(n_sub == n)
        def _(n=n):
            m = n * MOE_SUB
            if m:
                xv = x_ref[:m, :]
                a = _dot(xv, w1_ref[...].astype(BF16))
                b = _dot(xv, w3_ref[...].astype(BF16))
                o_ref[:m, :] = ((a * jax.nn.sigmoid(a)) * b).astype(o_ref.dtype)
            if m < MOE_BM:
                o_ref[m:, :] = jnp.zeros((MOE_BM - m, o_ref.shape[1]), o_ref.dtype)


def _moe_up(xs, w1, w3, moe_layer, tile_e, tile_rows, n_valid, bn=256):
    gj = D_FF_EXPERT // bn

    def row(t, nv):
        return jnp.minimum(t, nv[0] - 1)

    def col(t, j, nv):
        return jnp.where(t < nv[0], j, gj - 1)

    w_spec = pl.BlockSpec((None, None, D_MODEL, bn),
                          lambda t, j, te, nr, nv: (moe_layer, te[t], 0, col(t, j, nv)))
    grid_spec = pltpu.PrefetchScalarGridSpec(
        num_scalar_prefetch=3, grid=(MOE_TILES, gj),
        in_specs=[pl.BlockSpec((MOE_BM, D_MODEL), lambda t, j, te, nr, nv: (row(t, nv), 0)), w_spec, w_spec],
        out_specs=pl.BlockSpec((MOE_BM, bn), lambda t, j, te, nr, nv: (t, j)),
    )
    return pl.pallas_call(
        _moe_up_body, grid_spec=grid_spec,
        out_shape=jax.ShapeDtypeStruct((xs.shape[0], D_FF_EXPERT), BF16),
        compiler_params=_params(2), name="moe_up",
    )(tile_e, tile_rows, n_valid, xs, w1, w3)


def _moe_down_body(te_ref, nr_ref, nv_ref, x_ref, w_ref, o_ref):
    n_sub = _sub_blocks(nr_ref)
    first = pl.program_id(2) == 0
    for n in range(MOE_BM // MOE_SUB + 1):
        m = n * MOE_SUB

        @pl.when((n_sub == n) & first)
        def _(m=m):
            if m:
                o_ref[:m, :] = _dot(x_ref[:m, :], w_ref[...].astype(BF16))
            if m < MOE_BM:
                o_ref[m:, :] = jnp.zeros((MOE_BM - m, o_ref.shape[1]), o_ref.dtype)

        if m:
            @pl.when((n_sub == n) & jnp.logical_not(first))
            def _(m=m):
                o_ref[:m, :] += _dot(x_ref[:m, :], w_ref[...].astype(BF16))


def _moe_down(mid, w2, moe_layer, tile_e, tile_rows, n_valid, bn=1024, bk=2048):
    gj, gk = D_MODEL // bn, D_FF_EXPERT // bk

    def row(t, nv):
        return jnp.minimum(t, nv[0] - 1)

    def clamp(t, v, last, nv):
        return jnp.where(t < nv[0], v, last)

    grid_spec = pltpu.PrefetchScalarGridSpec(
        num_scalar_prefetch=3, grid=(MOE_TILES, gj, gk),
        in_specs=[
            pl.BlockSpec((MOE_BM, bk), lambda t, j, k, te, nr, nv: (row(t, nv), clamp(t, k, gk - 1, nv))),
            pl.BlockSpec((None, None, bk, bn),
                         lambda t, j, k, te, nr, nv: (moe_layer, te[t], clamp(t, k, gk - 1, nv),
                                                      clamp(t, j, gj - 1, nv))),
        ],
        out_specs=pl.BlockSpec((MOE_BM, bn), lambda t, j, k, te, nr, nv: (t, j)),
    )
    return pl.pallas_call(
        _moe_down_body, grid_spec=grid_spec,
        out_shape=jax.ShapeDtypeStruct((mid.shape[0], D_MODEL), F32),
        compiler_params=_params(3), name="moe_down",
    )(tile_e, tile_rows, n_valid, mid, w2)


def _combine_body(pos_ref, y_hbm, wt_ref, x_ref, g_ref, o_ref, buf, sem):
    base = pl.program_id(0) * GATHER_ROWS

    def row_copy(r, c, src_row):
        return pltpu.make_async_copy(y_hbm.at[pl.ds(src_row, 1)], buf.at[c, pl.ds(r, 1)], sem)

    def start(g, carry):
        for u in range(GATHER_UNROLL):
            r = g * GATHER_UNROLL + u
            for c in range(TOP_K):
                row_copy(r, c, pos_ref[(base + r) * TOP_K + c]).start(priority=c)
        return carry

    def wait(g, carry):
        for u in range(GATHER_UNROLL):
            for c in range(TOP_K):
                row_copy(g * GATHER_UNROLL + u, c, 0).wait()
        return carry

    lax.fori_loop(0, GATHER_ROWS // GATHER_UNROLL, start, 0)
    lax.fori_loop(0, GATHER_ROWS // GATHER_UNROLL, wait, 0)
    wt = wt_ref[...]
    f = wt[:, 0:1] * buf[0] + wt[:, 1:2] * buf[1]
    o_ref[...] = x_ref[...] + g_ref[...] * f


def _moe_combine(y, pos, wts, x, mods, k_gate):
    m = x.shape[0]
    row = pl.BlockSpec((GATHER_ROWS, D_MODEL), lambda i, p: (i, 0))
    grid_spec = pltpu.PrefetchScalarGridSpec(
        num_scalar_prefetch=1, grid=(m // GATHER_ROWS,),
        in_specs=[
            pl.BlockSpec(memory_space=pl.ANY),
            pl.BlockSpec((GATHER_ROWS, LANES), lambda i, p: (i, 0)),
            row,
            pl.BlockSpec((None, 1, D_MODEL), lambda i, p: (_cond_of_tile(i, GATHER_ROWS) * 6 + k_gate, 0, 0)),
        ],
        out_specs=row,
        scratch_shapes=[pltpu.VMEM((TOP_K, GATHER_ROWS, D_MODEL), F32), pltpu.SemaphoreType.DMA(())],
    )
    return pl.pallas_call(
        _combine_body, grid_spec=grid_spec,
        out_shape=jax.ShapeDtypeStruct((m, D_MODEL), F32),
        compiler_params=_params(1), name="moe_combine",
    )(pos, y, wts, x, mods)


def _rope_tables():
    n_rows = DEC_SEQ // GRID_W
    row = jnp.repeat(jnp.arange(n_rows), GRID_W).astype(F32)
    col = jnp.tile(jnp.arange(GRID_W), n_rows).astype(F32)
    half = QK_ROPE // 2
    inv = ROPE_BASE ** (-(jnp.arange(half // 2, dtype=F32) * 2.0) / half)
    ang_r = row[:, None] * inv
    ang_c = col[:, None] * inv
    ang = jnp.concatenate([ang_r, ang_r, ang_c, ang_c], axis=-1)
    pad1 = jnp.ones((DEC_SEQ, LANES - QK_ROPE), F32)
    cos = jnp.concatenate([jnp.cos(ang), pad1], axis=1)
    sin = jnp.concatenate([jnp.sin(ang), 0 * pad1], axis=1)
    return cos, sin


def _mod_spec(bm, bn, which):
    return ((None, 1, bn), lambda i, j: (_cond_of_tile(i, bm) * 6 + which, 0, j))


def kernel(x_prompt, x_sample, cache_ckv, cache_kpe, c, c_ctx, norm1_g, norm2_g, w_ada, b_ada, w_in,
           w_fourier_out, conv_w, w_conv_out, q_norm_g, w_uq, kv_norm_g, w_ukv, w_mla_out, w_o,
           ffn_w1, ffn_w3, ffn_w2, router_w, router_b, moe_w1, moe_w3, moe_w2, final_g):
    bm = 1024
    x = jnp.concatenate([x_prompt.reshape(N_PROMPT, D_MODEL), x_sample.reshape(N_SAMPLE, D_MODEL)], axis=0)
    w_in_t = jnp.swapaxes(w_in, 1, 2)

    cond8 = jnp.concatenate([c_ctx[None, :], c, jnp.zeros((8 - 1 - DEC_BATCH, D_MODEL), F32)], axis=0)
    mods_all = _ada(cond8, w_ada, b_ada)[:, :1 + DEC_BATCH].reshape(DEPTH, (1 + DEC_BATCH) * 6, 1, D_MODEL)

    cos, sin = _rope_tables()
    one = jnp.ones((N_PROMPT, LANES), F32)
    cos_q = jnp.concatenate([one] + [cos] * DEC_BATCH, axis=0)
    sin_q = jnp.concatenate([0 * one] + [sin] * DEC_BATCH, axis=0)
    one_p = jnp.ones((PAST_LEN, LANES), F32)
    cos_k = jnp.concatenate([one_p, cos] * DEC_BATCH + [one], axis=0)
    sin_k = jnp.concatenate([0 * one_p, sin] * DEC_BATCH + [0 * one], axis=0)
    tab_chan = _dft_tables(FOURIER_GROUP, 1.0)
    tab_prompt = _dft_tables(SEQ, -1.0)
    tab_sample = _dft_tables(DEC_SEQ, -1.0)

    ckv_out, kpe_out = [], []
    for layer in range(DEPTH):
        mods = mods_all[layer]
        uq = w_uq[layer].reshape(Q_LORA, N_HEADS, QK_NOPE + QK_ROPE)
        uq = jnp.concatenate([uq, jnp.zeros((Q_LORA, N_HEADS, HEAD_PAD - QK_NOPE - QK_ROPE), F32)], axis=-1)
        uq = uq.reshape(Q_LORA, N_HEADS * HEAD_PAD).astype(BF16)

        h = _norm_mod(x, norm1_g, layer, mods, 1, 0)
        z = _mm("in_proj", h, [(w_in_t, (layer,), 0, True)], _ep_plain, bm=bm, bn=512, n_out=Z_MAIN, out_dtype=F32)

        ab = _fourier_chan(z, tab_chan)
        yf_pre = jnp.concatenate([
            _fourier_seq(ab, tab_prompt, BATCH, SEQ, 0, SEQ, D_FOURIER),
            _fourier_seq(ab, tab_sample, DEC_BATCH, DEC_SEQ, N_PROMPT // DEC_SEQ, 512, 512)], axis=0)
        y_f = _mm("fourier_out", yf_pre, [(w_fourier_out, (layer,), 0, False)], _ep_plain,
                  bm=bm, bn=2048, n_out=D_MODEL, out_dtype=F32)

        yc_pre = _conv(z, conv_w, layer)
        y_c = _mm("conv_out", yc_pre, [(w_conv_out, (layer,), 0, False)], _ep_plain,
                  bm=bm, bn=2048, n_out=D_MODEL, out_dtype=F32)

        q_col = (D_FOURIER + 3 * D_CONV) // Q_LORA
        q = _mm("q_proj", z, [(uq, (), 0, False)], _ep_rope_q, bm=bm, bn=512, n_out=N_HEADS * HEAD_PAD,
                out_dtype=BF16, k=Q_LORA, x_kblk=q_col, prologue=_pro_rmsnorm,
                pro_extras=[(q_norm_g.reshape(DEPTH, 1, Q_LORA), (None, 1, Q_LORA), lambda i, j: (layer, 0, 0))],
                extras=[(cos_q, (bm, LANES), lambda i, j: (i, 0)), (sin_q, (bm, LANES), lambda i, j: (i, 0))])
        kv_col = (D_FOURIER + 3 * D_CONV + Q_LORA) // KV_LORA
        c_kv = _rmsnorm(z, kv_norm_g.reshape(DEPTH, 1, KV_LORA), layer, KV_LORA, kv_col)
        k_pe = z[:, GATE_COL0 - QK_ROPE:GATE_COL0]
        ckv_out.append(c_kv[:N_PROMPT].reshape(BATCH, SEQ, KV_LORA))
        kpe_out.append(k_pe[:N_PROMPT].reshape(BATCH, SEQ, QK_ROPE))
        ckv_s = jnp.concatenate([cache_ckv[:, layer], c_kv[N_PROMPT:].reshape(DEC_BATCH, DEC_SEQ, KV_LORA)], axis=1)
        ckv_ext = jnp.concatenate([ckv_s.reshape(-1, KV_LORA), c_kv[:N_PROMPT]], axis=0).astype(BF16)
        kpe_s = jnp.concatenate([cache_kpe[:, layer], k_pe[N_PROMPT:].reshape(DEC_BATCH, DEC_SEQ, QK_ROPE)], axis=1)
        kpe_ext = jnp.concatenate([kpe_s.reshape(-1, QK_ROPE), k_pe[:N_PROMPT]], axis=0)
        kpe_ext = jnp.concatenate([kpe_ext, jnp.zeros((N_KV_ROWS, LANES - QK_ROPE), F32)], axis=1)
        kpe_ext = _rope_k(kpe_ext, cos_k, sin_k)
        kv = _mm("kv_proj", ckv_ext, [(w_ukv, (layer,), 0, False)], _ep_plain,
                 bm=512, bn=2048, n_out=N_HEADS * (QK_NOPE + V_HEAD), out_dtype=BF16)
        attn = jnp.concatenate([
            _attention(q, kv, kpe_ext, BATCH, SEQ, SEQ, 0, DEC_BATCH * KV_LEN, SEQ, N_HEADS),
            _attention(q, kv, kpe_ext, DEC_BATCH, DEC_SEQ, KV_LEN, N_PROMPT, 0, 512, 8)], axis=0)
        y_a = _mm("mla_out", attn, [(w_mla_out, (layer,), 0, False)], _ep_plain,
                  bm=bm, bn=1024, n_out=D_MODEL, out_dtype=F32)

        tile = ((bm, 256), lambda i, j: (i, j))
        merged = _mm("gate_merge", h,
                     [(w_in_t, (layer,), GATE_COL0 + b * D_MODEL, True) for b in range(N_BRANCH)], _ep_merge,
                     bm=bm, bn=256, n_out=D_MODEL, out_dtype=BF16, x_buffers=1,
                     extras=[(y_f,) + tile, (y_c,) + tile, (y_a,) + tile])
        x = _mm("out_proj", merged, [(w_o, (layer,), 0, False)], _ep_residual, bm=bm, bn=512, n_out=D_MODEL,
                out_dtype=F32, extras=[(x, (bm, 512), lambda i, j: (i, j)), (mods,) + _mod_spec(bm, 512, 2)])

        j = layer // 2
        if layer % 2 == 0:
            h2 = _norm_mod(x, norm2_g, layer, mods, 4, 3)
            mid = _mm("ffn_up", h2, [(ffn_w1, (j,), 0, False), (ffn_w3, (j,), 0, False)], _ep_swiglu,
                      bm=bm, bn=256, n_out=D_FF, out_dtype=BF16)
            x = _mm("ffn_down", mid, [(ffn_w2[j].astype(BF16), (), 0, False)], _ep_residual, bm=512, bn=256,
                    n_out=D_MODEL, out_dtype=F32,
                    extras=[(x, (512, 256), lambda i, j: (i, j)), (mods,) + _mod_spec(512, 256, 5)])
        else:
            rw = jnp.concatenate([router_w[j], jnp.zeros((D_MODEL, LANES - N_EXPERTS), F32)], axis=1)
            rb = jnp.concatenate([router_b[j], jnp.zeros((LANES - N_EXPERTS,), F32)])[None, :]
            h2f, wts, idx = _norm_mod(x, norm2_g, layer, mods, 4, 3, router=_split_bf16(rw) + (rb,))
            pos, src, tile_e, tile_rows, n_valid = _route(idx[:, :TOP_K])
            xs = _gather_rows(h2f, src, tile_rows)
            mid = _moe_up(xs, moe_w1, moe_w3, j, tile_e, tile_rows, n_valid)
            y = _moe_down(mid, moe_w2, j, tile_e, tile_rows, n_valid)
            x = _moe_combine(y, pos, wts, x, mods, 5)

    fg = final_g.reshape(1, 1, D_MODEL)
    y_prompt = _rmsnorm(x, fg, 0, D_MODEL, 0, bm=256, row0=0, rows=N_PROMPT)
    y_sample = _rmsnorm(x, fg, 0, D_MODEL, 0, bm=256, row0=N_PROMPT, rows=N_SAMPLE)
    return (y_prompt.reshape(BATCH, SEQ, D_MODEL),
            y_sample.reshape(DEC_BATCH, DEC_SEQ, D_MODEL),
            jnp.stack(ckv_out, axis=1),
            jnp.stack(kpe_out, axis=1))
```
